```python
import jax, jax.numpy as jnp
from jax import lax
import numpy as np

D_MODEL = 1024
BATCH = 8
SEQ = 8192
DEPTH = 4
DEC_BATCH = 32
DEC_SEQ = 64
PAST_LEN = 4096

CHUNK = 64
E_CONV = 1024
G_CONV = 8
E_SGU = 1024
G_SGU = 8
SGU_HEAD = E_SGU // G_SGU
SGU_CHUNK = 128
CONV_W = 3
MIX_WIDTH = E_CONV + E_SGU
IN_COLS = 4 * E_CONV + 3 * E_SGU
EPS = 1e-6

kernel_name = "hymba_shortconv_sgu_streaming_step"


def rmsnorm(x, g):
    xf = x.astype(jnp.float32)
    y = xf * lax.rsqrt(jnp.mean(xf * xf, axis=-1, keepdims=True) + EPS)
    return (y * g.astype(jnp.float32)).astype(x.dtype)


def layernorm(x, g, b):
    xf = x.astype(jnp.float32)
    mu = jnp.mean(xf, axis=-1, keepdims=True)
    xc = xf - mu
    var = jnp.mean(xc * xc, axis=-1, keepdims=True)
    y = xc * lax.rsqrt(var + EPS) * g.astype(jnp.float32) + b.astype(jnp.float32)
    return y.astype(x.dtype)


def short_conv(xc, prev, w):
    T = xc.shape[1]
    xp = jnp.concatenate([prev.astype(xc.dtype), xc], axis=1)
    y = w[0] * xp[:, 0:T]
    for k in range(1, CONV_W):
        y = y + w[k] * xp[:, k:k + T]
    return y, xp[:, xp.shape[1] - (CONV_W - 1):]


def spatial_gate(u, v, w_s, b_s):
    Bn, T, _ = v.shape
    n = -(-T // SGU_CHUNK)
    pad = n * SGU_CHUNK - T
    vp = jnp.pad(v, ((0, 0), (0, pad), (0, 0))).reshape(Bn, n, SGU_CHUNK, G_SGU, SGU_HEAD)
    blk = jnp.arange(SGU_CHUNK) // CHUNK
    mask = (blk[:, None] >= blk[None, :]).astype(w_s.dtype)
    mixed = jnp.einsum('gts,bnsgc->bntgc', w_s * mask, vp)
    mixed = mixed + b_s.T[None, None, :, :, None]
    mixed = mixed.reshape(Bn, n * SGU_CHUNK, E_SGU)[:, :T]
    return u * mixed


def mixer_layer(x, conv_prev, norm_g, w_in, conv_w, sg_ln_g, sg_ln_b, sg_w, sg_b,
                out_g_conv, out_g_sgu, w_out):
    xn = rmsnorm(x, norm_g)
    proj = jnp.einsum('btd,de->bte', xn, w_in)
    splits = [E_CONV, 2 * E_CONV, 3 * E_CONV, 4 * E_CONV,
              4 * E_CONV + E_SGU, 4 * E_CONV + 2 * E_SGU]
    h, bg, cg, za, u, v, zb = jnp.split(proj, splits, axis=-1)
    yc, conv_state = short_conv(cg * h, conv_prev, conv_w)
    ya = rmsnorm(bg * yc, out_g_conv) * jax.nn.silu(za)
    vn = layernorm(v, sg_ln_g, sg_ln_b)
    yb = rmsnorm(spatial_gate(u, vn, sg_w, sg_b), out_g_sgu) * jax.nn.silu(zb)
    y = jnp.einsum('bte,ed->btd', jnp.concatenate([ya, yb], axis=-1), w_out)
    return x + y, conv_state, vn


def setup_inputs(seed: int = 0) -> dict:
    key = jax.random.key(seed)
    ks = jax.random.split(key, 16)
    f32 = jnp.float32
    nrm = lambda k, shape, s: jax.random.normal(k, shape, f32) * s
    return {
        "x_prompt": nrm(ks[0], (BATCH, SEQ, D_MODEL), 1.0),
        "x_sample": nrm(ks[1], (DEC_BATCH, DEC_SEQ, D_MODEL), 1.0),
        "state_conv": nrm(ks[2], (DEPTH, DEC_BATCH, CONV_W - 1, E_CONV), 1.0),
        "norm_g": 1.0 + nrm(ks[3], (DEPTH, D_MODEL), 0.02),
        "w_in": nrm(ks[4], (DEPTH, D_MODEL, IN_COLS), D_MODEL ** -0.5),
        "conv_w": nrm(ks[5], (DEPTH, CONV_W, E_CONV), CONV_W ** -0.5),
        "sg_ln_g": 1.0 + nrm(ks[6], (DEPTH, E_SGU), 0.02),
        "sg_ln_b": nrm(ks[7], (DEPTH, E_SGU), 0.02),
        "sg_w": nrm(ks[8], (DEPTH, G_SGU, SGU_CHUNK, SGU_CHUNK), 0.5 * SGU_CHUNK ** -0.5),
        "sg_b": 1.0 + nrm(ks[9], (DEPTH, G_SGU, SGU_CHUNK), 0.1),
        "out_g_conv": 1.0 + nrm(ks[10], (DEPTH, E_CONV), 0.02),
        "out_g_sgu": 1.0 + nrm(ks[11], (DEPTH, E_SGU), 0.02),
        "w_out": nrm(ks[12], (DEPTH, MIX_WIDTH, D_MODEL), 0.5 * MIX_WIDTH ** -0.5),
        "final_g": 1.0 + nrm(ks[13], (D_MODEL,), 0.02),
    }


def reference(x_prompt, x_sample, state_conv, norm_g, w_in, conv_w, sg_ln_g, sg_ln_b,
              sg_w, sg_b, out_g_conv, out_g_sgu, w_out, final_g):
    xp = x_prompt
    zero_prev = jnp.zeros((x_prompt.shape[0], CONV_W - 1, E_CONV), x_prompt.dtype)
    conv_p = []
    for l in range(DEPTH):
        xp, cs, _ = mixer_layer(xp, zero_prev, norm_g[l], w_in[l], conv_w[l], sg_ln_g[l],
                                sg_ln_b[l], sg_w[l], sg_b[l], out_g_conv[l], out_g_sgu[l],
                                w_out[l])
        conv_p.append(cs)
    y_prompt = rmsnorm(xp, final_g)

    xs = x_sample
    conv_s, v_s = [], []
    for l in range(DEPTH):
        xs, cs, vn = mixer_layer(xs, state_conv[l], norm_g[l], w_in[l], conv_w[l], sg_ln_g[l],
                                 sg_ln_b[l], sg_w[l], sg_b[l], out_g_conv[l], out_g_sgu[l],
                                 w_out[l])
        conv_s.append(cs)
        v_s.append(vn)
    y_sample = rmsnorm(xs, final_g)

    new_conv_prompt = jnp.stack(conv_p, axis=0)
    new_conv_sample = jnp.stack(conv_s, axis=0)
    new_sgu_v_sample = jnp.stack(v_s, axis=0)
    return (y_prompt, y_sample, new_conv_prompt, new_conv_sample, new_sgu_v_sample)
```

```python
import functools

import jax
import jax.numpy as jnp
from jax import lax
from jax.experimental import pallas as pl
from jax.experimental.pallas import tpu as pltpu

D_MODEL = 1024
E_CONV = 1024
E_SGU = 1024
G_SGU = 8
SGU_HEAD = E_SGU // G_SGU
SGU_CHUNK = 128
CHUNK = 64
CONV_W = 3
N_PROJ = 7
EPS = 1e-6

SUBLANES = 8
TILE_ROWS = 256
VMEM_LIMIT_BYTES = 48 * 1024 * 1024

_F32 = jnp.float32
_BF16 = jnp.bfloat16


def _rms(v, g):
    return v * lax.rsqrt(jnp.mean(v * v, axis=-1, keepdims=True) + EPS) * g


def _layer_kernel(x_ref, prev_ref, ng_ref, win_ref, cw_ref, lng_ref, lnb_ref, sgw_ref,
                  sgb_ref, gcv_ref, gsg_ref, wout_ref, fing_ref,
                  y_ref, cs_ref, *rest, nseg, seglen, ch, final, emit_vn):
    if emit_vn:
        vn_ref, xc_buf = rest
    else:
        (xc_buf,) = rest
    t = pl.program_id(1)
    rows = nseg * seglen
    hist = SUBLANES - (CONV_W - 1)

    x = x_ref[...]
    xn = _rms(x, ng_ref[...]).astype(_BF16)

    def proj(k):
        return jnp.dot(xn, win_ref[:, k * E_CONV:(k + 1) * E_CONV],
                       preferred_element_type=_F32)

    h, bg, cg, za = proj(0), proj(1), proj(2), proj(3)
    xc = cg * h

    @pl.when(t == 0)
    def _():
        xc_buf[:, hist:SUBLANES, :] = prev_ref[...]

    @pl.when(t > 0)
    def _():
        xc_buf[:, hist:SUBLANES, :] = xc_buf[:, seglen + hist:seglen + SUBLANES, :]

    xc_buf[:, SUBLANES:SUBLANES + seglen, :] = xc.reshape(nseg, seglen, E_CONV)
    xm1 = xc_buf[:, SUBLANES - 1:SUBLANES - 1 + seglen, :].reshape(rows, E_CONV)
    xm2 = xc_buf[:, SUBLANES - 2:SUBLANES - 2 + seglen, :].reshape(rows, E_CONV)
    cs_ref[...] = xc_buf[:, seglen + hist:seglen + SUBLANES, :]
    cw = cw_ref[...]
    yc = cw[0:1] * xm2 + cw[1:2] * xm1 + cw[2:3] * xc
    ya = _rms(bg * yc, gcv_ref[...]) * jax.nn.silu(za)

    u, v, zb = proj(4), proj(5), proj(6)
    mu = jnp.mean(v, axis=-1, keepdims=True)
    vc = v - mu
    var = jnp.mean(vc * vc, axis=-1, keepdims=True)
    vn = vc * lax.rsqrt(var + EPS) * lng_ref[...] + lnb_ref[...]
    if emit_vn:
        vn_ref[...] = vn
    vnb = vn.astype(_BF16)
    nch = rows // ch
    blk_t = lax.broadcasted_iota(jnp.int32, (ch, ch), 0) // CHUNK
    blk_s = lax.broadcasted_iota(jnp.int32, (ch, ch), 1) // CHUNK
    mask = (blk_t >= blk_s).astype(_F32)
    cols = []
    for g in range(G_SGU):
        wg = (sgw_ref[g, 0:ch, 0:ch] * mask).astype(_BF16)
        rhs = jnp.concatenate(
            [vnb[c * ch:(c + 1) * ch, g * SGU_HEAD:(g + 1) * SGU_HEAD] for c in range(nch)],
            axis=1)
        m = jnp.dot(wg, rhs, preferred_element_type=_F32)
        cols.append(jnp.concatenate(
            [m[:, c * SGU_HEAD:(c + 1) * SGU_HEAD] for c in range(nch)], axis=0))
    bias = sgb_ref[0:ch, :]
    mixed = jnp.concatenate(cols, axis=1) + jnp.concatenate([bias] * nch, axis=0)
    yb = _rms(u * mixed, gsg_ref[...]) * jax.nn.silu(zb)

    ycat = jnp.concatenate([ya.astype(_BF16), yb.astype(_BF16)], axis=1)
    out = x + jnp.dot(ycat, wout_ref[...], preferred_element_type=_F32)
    if final:
        out = _rms(out, fing_ref[...])
    y_ref[...] = out


def _resident(shape):
    return pl.BlockSpec(shape, lambda b, t: (0,) * len(shape), pipeline_mode=pl.Buffered(1))


def _layer(x, prev, lw, *, nseg, seglen, ch, final, emit_vn, name):
    nb, length, _ = x.shape
    rows = nseg * seglen
    nt = length // rows
    assert nt * rows == length and (nseg == 1 or nt == 1)
    row_spec = pl.BlockSpec((None, rows, D_MODEL), lambda b, t: (b, t, 0))
    state_spec = pl.BlockSpec((None, nseg, CONV_W - 1, E_CONV), lambda b, t: (b, 0, 0, 0))
    out_shape = [jax.ShapeDtypeStruct(x.shape, _F32),
                 jax.ShapeDtypeStruct(prev.shape, _F32)]
    out_specs = [row_spec, state_spec]
    if emit_vn:
        out_shape.append(jax.ShapeDtypeStruct((nb, length, E_SGU), _F32))
        out_specs.append(pl.BlockSpec((None, rows, E_SGU), lambda b, t: (b, t, 0)))
    kern = functools.partial(_layer_kernel, nseg=nseg, seglen=seglen, ch=ch, final=final,
                             emit_vn=emit_vn)
    return pl.pallas_call(
        kern,
        grid=(nb, nt),
        in_specs=[
            row_spec, state_spec,
            _resident((1, D_MODEL)),
            _resident((D_MODEL, N_PROJ * E_CONV)),
            _resident((CONV_W, E_CONV)),
            _resident((1, E_SGU)), _resident((1, E_SGU)),
            _resident((G_SGU, SGU_CHUNK, SGU_CHUNK)),
            _resident((SGU_CHUNK, E_SGU)),
            _resident((1, E_CONV)), _resident((1, E_SGU)),
            _resident((E_CONV + E_SGU, D_MODEL)),
            _resident((1, D_MODEL)),
        ],
        out_specs=out_specs,
        out_shape=out_shape,
        scratch_shapes=[pltpu.VMEM((nseg, SUBLANES + seglen, E_CONV), _F32)],
        compiler_params=pltpu.CompilerParams(
            dimension_semantics=("arbitrary", "arbitrary"),
            vmem_limit_bytes=VMEM_LIMIT_BYTES),
        name=name,
    )(x, prev, *lw)


def kernel(x_prompt, x_sample, state_conv, norm_g, w_in, conv_w, sg_ln_g, sg_ln_b, sg_w, sg_b,
           out_g_conv, out_g_sgu, w_out, final_g):
    depth = w_in.shape[0]
    batch, seq, _ = x_prompt.shape
    dec_batch, dec_seq, _ = x_sample.shape
    seqs_per_tile = TILE_ROWS // dec_seq
    dec_groups = dec_batch // seqs_per_tile

    def layer_weights(l):
        row = lambda a: a[l].reshape(1, -1)
        bias = jnp.repeat(sg_b[l].T, SGU_HEAD, axis=1)
        return (row(norm_g), w_in[l].astype(_BF16), conv_w[l], row(sg_ln_g), row(sg_ln_b),
                sg_w[l], bias, row(out_g_conv), row(out_g_sgu), w_out[l].astype(_BF16),
                final_g.reshape(1, -1))

    xp = x_prompt
    xs = x_sample.reshape(dec_groups, seqs_per_tile * dec_seq, D_MODEL)
    zero_prev = jnp.zeros((batch, 1, CONV_W - 1, E_CONV), _F32)
    conv_p, conv_s, v_s = [], [], []
    for l in range(depth):
        lw = layer_weights(l)
        final = l == depth - 1
        xp, cs = _layer(xp, zero_prev, lw, nseg=1, seglen=TILE_ROWS, ch=SGU_CHUNK,
                        final=final, emit_vn=False, name=f"prompt_layer{l}")
        conv_p.append(cs.reshape(batch, CONV_W - 1, E_CONV))
        prev = state_conv[l].reshape(dec_groups, seqs_per_tile, CONV_W - 1, E_CONV)
        xs, cs, vn = _layer(xs, prev, lw, nseg=seqs_per_tile, seglen=dec_seq, ch=dec_seq,
                            final=final, emit_vn=True, name=f"sample_layer{l}")
        conv_s.append(cs.reshape(dec_batch, CONV_W - 1, E_CONV))
        v_s.append(vn.reshape(dec_batch, dec_seq, E_SGU))
    return (xp, xs.reshape(x_sample.shape), jnp.stack(conv_p, axis=0),
            jnp.stack(conv_s, axis=0), jnp.stack(v_s, axis=0))
```

```python
import functools

import jax
import jax.numpy as jnp
from jax import lax
from jax.experimental import pallas as pl
from jax.experimental.pallas import tpu as pltpu

D_MODEL = 1024
E_CONV = 1024
E_SGU = 1024
G_SGU = 8
SGU_HEAD = E_SGU // G_SGU
SGU_CHUNK = 128
CHUNK = 64
CONV_W = 3
N_PROJ = 7
EPS = 1e-6

SUBLANES = 8
BF16_ROWS = 16
TILE_ROWS = 256
CONV_SLABS = 4
GATE_SLABS = 2
VMEM_LIMIT_BYTES = 56 * 1024 * 1024

_F32 = jnp.float32
_BF16 = jnp.bfloat16


def _rms(v, g):
    return v * lax.rsqrt(jnp.mean(v * v, axis=-1, keepdims=True) + EPS) * g


def _after(v, done):
    bits = pltpu.bitcast(done, jnp.int32)
    acc = bits[0:SUBLANES]
    for j in range(1, bits.shape[0] // SUBLANES):
        acc = acc | bits[j * SUBLANES:(j + 1) * SUBLANES]
    zero = lax.shift_right_logical(lax.shift_right_logical(acc, 16), 16).astype(_F32)
    return v + jnp.concatenate([zero] * (v.shape[0] // SUBLANES), axis=0)


def _layer_step(xf_ref, xb_ref, prev_ref, ng_ref, win_ref, cw_ref, lng_ref, lnb_ref,
                sgw_ref, sgb_ref, gcv_ref, gsg_ref, wout_ref, fing_ref,
                y_ref, cs_ref, vn_ref, xc_buf, mixed_buf, ycat_buf, p_front, p_back,
                seq_start, *, nseg, seglen, ch, final):
    rows = nseg * seglen
    hist = SUBLANES - (CONV_W - 1)
    nch = rows // ch

    xn = _rms(xf_ref[...], ng_ref[...])
    xn_head = xn[0:BF16_ROWS]
    xn_tail = xn[BF16_ROWS:].astype(_BF16)

    def project(k, done=None):
        head = xn_head if done is None else _after(xn_head, done)
        lhs = jnp.concatenate([head.astype(_BF16), xn_tail], axis=0)
        res = jnp.dot(lhs, win_ref[:, k * E_CONV:(k + 1) * E_CONV],
                      preferred_element_type=_F32)
        p_front[k] = res
        return res[rows - SUBLANES:rows]

    def conv_slab(j, start):
        n = rows // CONV_SLABS
        r0 = j * n
        s, o = r0 // seglen, r0 % seglen
        h, bg, cg, za = (p_back[k, r0:r0 + n, :] for k in range(4))
        h, za = _after(h, start), _after(za, start)
        xc = cg * h
        xc_buf[s, SUBLANES + o:SUBLANES + o + n, :] = xc
        xm1 = xc_buf[s, SUBLANES - 1 + o:SUBLANES - 1 + o + n, :]
        xm2 = xc_buf[s, SUBLANES - 2 + o:SUBLANES - 2 + o + n, :]
        cw = cw_ref[...]
        yc = cw[0:1] * xm2 + cw[1:2] * xm1 + cw[2:3] * xc
        ya = (_rms(bg * yc, gcv_ref[...]) * jax.nn.silu(za)).astype(_BF16)
        ycat_buf[r0:r0 + n, 0:E_CONV] = ya
        return ya

    def sgu_mix():
        v = p_back[5]
        mu = jnp.mean(v, axis=-1, keepdims=True)
        vc = v - mu
        var = jnp.mean(vc * vc, axis=-1, keepdims=True)
        vn = vc * lax.rsqrt(var + EPS) * lng_ref[...] + lnb_ref[...]
        if vn_ref is not None:
            vn_ref[...] = vn
        vnb = vn.astype(_BF16)
        blk_t = lax.broadcasted_iota(jnp.int32, (ch, ch), 0) // CHUNK
        blk_s = lax.broadcasted_iota(jnp.int32, (ch, ch), 1) // CHUNK
        mask = (blk_t >= blk_s).astype(_F32)
        for g in range(G_SGU):
            wg = (sgw_ref[g, 0:ch, 0:ch] * mask).astype(_BF16)
            rhs = jnp.concatenate(
                [vnb[c * ch:(c + 1) * ch, g * SGU_HEAD:(g + 1) * SGU_HEAD]
                 for c in range(nch)], axis=1)
            m = jnp.dot(wg, rhs, preferred_element_type=_F32)
            for c in range(nch):
                mixed_buf[c * ch:(c + 1) * ch, g * SGU_HEAD:(g + 1) * SGU_HEAD] = (
                    m[:, c * SGU_HEAD:(c + 1) * SGU_HEAD])

    def gate_slab(j, start):
        n = rows // GATE_SLABS
        r0 = j * n
        u, zb = p_back[4, r0:r0 + n, :], p_back[6, r0:r0 + n, :]
        u, zb = _after(u, start), _after(zb, start)
        bias = jnp.concatenate([sgb_ref[0:ch, :]] * (n // ch), axis=0)
        mixed = mixed_buf[r0:r0 + n, :] + bias
        yb = (_rms(u * mixed, gsg_ref[...]) * jax.nn.silu(zb)).astype(_BF16)
        ycat_buf[r0:r0 + n, E_CONV:E_CONV + E_SGU] = yb
        return yb

    carried = xc_buf[:, seglen + hist:seglen + SUBLANES, :]
    xc_buf[:, hist:SUBLANES, :] = jnp.where(seq_start, prev_ref[...], carried)

    tails = [project(0)]
    sgu_mix()
    tails.append(project(1))
    for j in range(CONV_SLABS):
        tails.append(project(2 + j, done=conv_slab(j, tails[j])))
    cs_ref[...] = xc_buf[:, seglen + hist:seglen + SUBLANES, :]
    tails.append(project(6, done=gate_slab(0, tails[4])))
    gate_slab(1, tails[5])

    out = xb_ref[...] + jnp.dot(ycat_buf[...], wout_ref[...], preferred_element_type=_F32)
    if final:
        out = _rms(out, fing_ref[...])
    y_ref[...] = out


def _layer_kernel(xf_ref, xb_ref, prev_ref, ng_ref, win_ref, cw_ref, lng_ref, lnb_ref,
                  sgw_ref, sgb_ref, gcv_ref, gsg_ref, wout_ref, fing_ref,
                  y_ref, cs_ref, *rest, tiles_per_seq, emit_vn, **static):
    if emit_vn:
        vn_ref, *scratch = rest
    else:
        vn_ref, scratch = None, rest
    xc_buf, mixed_buf, ycat_buf, p_even, p_odd = scratch
    i = pl.program_id(0)
    seq_start = jnp.logical_or(i == 0, (i - 1) % tiles_per_seq == 0)

    @pl.when(i == 0)
    def _():
        p_odd[...] = jnp.zeros(p_odd.shape, _F32)
        xc_buf[...] = jnp.zeros(xc_buf.shape, _F32)

    step = functools.partial(
        _layer_step, xf_ref, xb_ref, prev_ref, ng_ref, win_ref, cw_ref, lng_ref, lnb_ref,
        sgw_ref, sgb_ref, gcv_ref, gsg_ref, wout_ref, fing_ref, y_ref, cs_ref, vn_ref,
        xc_buf, mixed_buf, ycat_buf, seq_start=seq_start, **static)

    @pl.when(i % 2 == 0)
    def _():
        step(p_front=p_even, p_back=p_odd)

    @pl.when(i % 2 == 1)
    def _():
        step(p_front=p_odd, p_back=p_even)


def _resident(shape):
    return pl.BlockSpec(shape, lambda i: (0,) * len(shape), pipeline_mode=pl.Buffered(1))


def _layer(x, prev, lw, *, nseg, seglen, ch, tiles_per_seq, final, emit_vn, name):
    n_rows = x.shape[0]
    rows = nseg * seglen
    n_tiles = n_rows // rows
    assert n_tiles * rows == n_rows and (nseg == 1 or tiles_per_seq == 1)
    assert prev.shape[0] * tiles_per_seq == n_tiles
    conv_slab, gate_slab = rows // CONV_SLABS, rows // GATE_SLABS
    assert seglen % conv_slab == 0 and gate_slab % ch == 0 and rows % ch == 0
    mixed_tile = lambda i: jnp.maximum(i - 1, 0)
    front_spec = pl.BlockSpec((rows, D_MODEL), lambda i: (jnp.minimum(i, n_tiles - 1), 0))
    back_spec = pl.BlockSpec((rows, D_MODEL), lambda i: (mixed_tile(i), 0))
    state_spec = pl.BlockSpec((None, nseg, CONV_W - 1, E_CONV),
                              lambda i: (mixed_tile(i) // tiles_per_seq, 0, 0, 0))
    out_shape = [jax.ShapeDtypeStruct(x.shape, _F32),
                 jax.ShapeDtypeStruct(prev.shape, _F32)]
    out_specs = [back_spec, state_spec]
    if emit_vn:
        out_shape.append(jax.ShapeDtypeStruct((n_rows, E_SGU), _F32))
        out_specs.append(back_spec)
    kern = functools.partial(_layer_kernel, nseg=nseg, seglen=seglen, ch=ch,
                             tiles_per_seq=tiles_per_seq, final=final, emit_vn=emit_vn)
    proj_buf = pltpu.VMEM((N_PROJ, rows, E_CONV), _F32)
    return pl.pallas_call(
        kern,
        grid=(n_tiles + 1,),
        in_specs=[
            front_spec, back_spec, state_spec,
            _resident((1, D_MODEL)),
            _resident((D_MODEL, N_PROJ * E_CONV)),
            _resident((CONV_W, E_CONV)),
            _resident((1, E_SGU)), _resident((1, E_SGU)),
            _resident((G_SGU, SGU_CHUNK, SGU_CHUNK)),
            _resident((SGU_CHUNK, E_SGU)),
            _resident((1, E_CONV)), _resident((1, E_SGU)),
            _resident((E_CONV + E_SGU, D_MODEL)),
            _resident((1, D_MODEL)),
        ],
        out_specs=out_specs,
        out_shape=out_shape,
        scratch_shapes=[pltpu.VMEM((nseg, SUBLANES + seglen, E_CONV), _F32),
                        pltpu.VMEM((rows, E_SGU), _F32),
                        pltpu.VMEM((rows, E_CONV + E_SGU), _BF16),
                        proj_buf, proj_buf],
        compiler_params=pltpu.CompilerParams(
            dimension_semantics=("arbitrary",),
            vmem_limit_bytes=VMEM_LIMIT_BYTES),
        name=name,
    )(x, x, prev, *lw)


def kernel(x_prompt, x_sample, state_conv, norm_g, w_in, conv_w, sg_ln_g, sg_ln_b, sg_w, sg_b,
           out_g_conv, out_g_sgu, w_out, final_g):
    depth = w_in.shape[0]
    batch, seq, _ = x_prompt.shape
    dec_batch, dec_seq, _ = x_sample.shape
    assert dec_seq <= CHUNK and TILE_ROWS % dec_seq == 0 and seq % TILE_ROWS == 0
    seqs_per_tile = TILE_ROWS // dec_seq
    dec_groups = dec_batch // seqs_per_tile

    def layer_weights(l):
        row = lambda a: a[l].reshape(1, -1)
        bias = jnp.repeat(sg_b[l].T, SGU_HEAD, axis=1)
        return (row(norm_g), w_in[l].astype(_BF16), conv_w[l], row(sg_ln_g), row(sg_ln_b),
                sg_w[l], bias, row(out_g_conv), row(out_g_sgu), w_out[l].astype(_BF16),
                final_g.reshape(1, -1))

    xp = x_prompt.reshape(batch * seq, D_MODEL)
    xs = x_sample.reshape(dec_batch * dec_seq, D_MODEL)
    zero_prev = jnp.zeros((batch, 1, CONV_W - 1, E_CONV), _F32)
    conv_p, conv_s, v_s = [], [], []
    for l in range(depth):
        lw = layer_weights(l)
        final = l == depth - 1
        xp, cs = _layer(xp, zero_prev, lw, nseg=1, seglen=TILE_ROWS, ch=SGU_CHUNK,
                        tiles_per_seq=seq // TILE_ROWS, final=final, emit_vn=False,
                        name=f"prompt_layer{l}")
        conv_p.append(cs.reshape(batch, CONV_W - 1, E_CONV))
        prev = state_conv[l].reshape(dec_groups, seqs_per_tile, CONV_W - 1, E_CONV)
        xs, cs, vn = _layer(xs, prev, lw, nseg=seqs_per_tile, seglen=dec_seq, ch=dec_seq,
                            tiles_per_seq=1, final=final, emit_vn=True,
                            name=f"sample_layer{l}")
        conv_s.append(cs.reshape(dec_batch, CONV_W - 1, E_CONV))
        v_s.append(vn.reshape(dec_batch, dec_seq, E_SGU))
    return (xp.reshape(x_prompt.shape), xs.reshape(x_sample.shape), jnp.stack(conv_p, axis=0),
            jnp.stack(conv_s, axis=0), jnp.stack(v_s, axis=0))
```

```python
import functools

import jax
import jax.numpy as jnp
from jax import lax
from jax.experimental import pallas as pl
from jax.experimental.pallas import tpu as pltpu

D_MODEL = 1024
E_CONV = 1024
E_SGU = 1024
G_SGU = 8
SGU_HEAD = E_SGU // G_SGU
SGU_CHUNK = 128
CHUNK = 64
CONV_W = 3
N_PROJ = 7
EPS = 1e-6

SUBLANES = 8
BF16_ROWS = 16
TILE_ROWS = 256
CONV_SLABS = 4
GATE_SLABS = 2
VMEM_LIMIT_BYTES = 56 * 1024 * 1024

_F32 = jnp.float32
_BF16 = jnp.bfloat16


def _rms(v, g):
    return v * lax.rsqrt(jnp.mean(v * v, axis=-1, keepdims=True) + EPS) * g


def _after(v, done):
    bits = pltpu.bitcast(done, jnp.int32)
    parts = [bits[j:j + SUBLANES] for j in range(0, bits.shape[0], SUBLANES)]
    while len(parts) > 1:
        parts = [a | b for a, b in zip(parts[0::2], parts[1::2])] + parts[len(parts) & ~1:]
    zero = lax.shift_right_logical(lax.shift_right_logical(parts[0], 16), 16).astype(_F32)
    return v + jnp.concatenate([zero] * (v.shape[0] // SUBLANES), axis=0)


def _layer_step(xf_ref, xb_ref, prev_ref, ng_ref, win_ref, cw_ref, lng_ref, lnb_ref,
                sgw_ref, sgb_ref, gcv_ref, gsg_ref, wout_ref, fing_ref,
                y_ref, cs_ref, vn_ref, hist_buf, mixed_buf, ycat_buf, p_front, p_back,
                seq_start, *, nseg, seglen, ch, final):
    rows = nseg * seglen
    hist = SUBLANES - (CONV_W - 1)
    nch = rows // ch

    xn = _rms(xf_ref[...], ng_ref[...])
    xn_head = xn[0:BF16_ROWS]
    xn_tail = xn[BF16_ROWS:].astype(_BF16)

    def project(k, done=None):
        head = xn_head if done is None else _after(xn_head, done)
        lhs = jnp.concatenate([head.astype(_BF16), xn_tail], axis=0)
        w = pltpu.bitcast(win_ref[:, k * E_CONV:(k + 1) * E_CONV], _BF16)
        res = jnp.dot(lhs, w, preferred_element_type=_F32)
        p_front[k] = res
        return res[rows - SUBLANES:rows]

    def conv_slab(j, start):
        n = rows // CONV_SLABS
        r0 = j * n
        s = r0 // seglen
        h, bg, cg, za = (p_back[k, r0:r0 + n, :] for k in range(4))
        xc = cg * _after(h, start)
        ext = jnp.concatenate([hist_buf[s], xc], axis=0)
        xm1 = pltpu.roll(ext, 1, axis=0)[SUBLANES:]
        xm2 = pltpu.roll(ext, 2, axis=0)[SUBLANES:]
        hist_buf[s] = xc[n - SUBLANES:n]
        cw = cw_ref[...]
        yc = cw[0:1] * xm2 + cw[1:2] * xm1 + cw[2:3] * xc
        ya = (_rms(bg * yc, gcv_ref[...]) * jax.nn.silu(za)).astype(_BF16)
        ycat_buf[r0:r0 + n, 0:E_CONV] = ya
        return ya

    def sgu_mix():
        v = p_back[5]
        mu = jnp.mean(v, axis=-1, keepdims=True)
        vc = v - mu
        var = jnp.mean(vc * vc, axis=-1, keepdims=True)
        vn = vc * lax.rsqrt(var + EPS) * lng_ref[...] + lnb_ref[...]
        if vn_ref is not None:
            vn_ref[...] = vn
        vnb = vn.astype(_BF16)
        blk_t = lax.broadcasted_iota(jnp.int32, (ch, ch), 0) // CHUNK
        blk_s = lax.broadcasted_iota(jnp.int32, (ch, ch), 1) // CHUNK
        mask = (blk_t >= blk_s).astype(_F32)
        for g in range(G_SGU):
            wg = (sgw_ref[g, 0:ch, 0:ch] * mask).astype(_BF16)
            rhs = jnp.concatenate(
                [vnb[c * ch:(c + 1) * ch, g * SGU_HEAD:(g + 1) * SGU_HEAD]
                 for c in range(nch)], axis=1)
            m = jnp.dot(wg, rhs, preferred_element_type=_F32)
            for c in range(nch):
                mixed_buf[c * ch:(c + 1) * ch, g * SGU_HEAD:(g + 1) * SGU_HEAD] = (
                    m[:, c * SGU_HEAD:(c + 1) * SGU_HEAD])

    def gate_slab(j, start):
        n = rows // GATE_SLABS
        r0 = j * n
        u, zb = p_back[4, r0:r0 + n, :], p_back[6, r0:r0 + n, :]
        u = _after(u, start)
        bias = jnp.concatenate([sgb_ref[0:ch, :]] * (n // ch), axis=0)
        mixed = mixed_buf[r0:r0 + n, :] + bias
        yb = (_rms(u * mixed, gsg_ref[...]) * jax.nn.silu(zb)).astype(_BF16)
        ycat_buf[r0:r0 + n, E_CONV:E_CONV + E_SGU] = yb
        return yb

    carried = hist_buf[:, hist:SUBLANES, :]
    hist_buf[:, hist:SUBLANES, :] = jnp.where(seq_start, prev_ref[...], carried)

    tails = [project(0)]
    sgu_mix()
    tails.append(project(1))
    for j in range(CONV_SLABS):
        tails.append(project(2 + j, done=conv_slab(j, tails[j])))
    cs_ref[...] = hist_buf[:, hist:SUBLANES, :]
    tails.append(project(6, done=gate_slab(0, tails[4])))
    gate_slab(1, tails[5])

    out = xb_ref[...] + jnp.dot(ycat_buf[...], pltpu.bitcast(wout_ref[...], _BF16),
                               preferred_element_type=_F32)
    if final:
        out = _rms(out, fing_ref[...])
    y_ref[...] = out


def _layer_kernel(xf_ref, xb_ref, prev_ref, ng_ref, win_ref, cw_ref, lng_ref, lnb_ref,
                  sgw_ref, sgb_ref, gcv_ref, gsg_ref, wout_ref, fing_ref,
                  y_ref, cs_ref, *rest, tiles_per_seq, emit_vn, **static):
    if emit_vn:
        vn_ref, *scratch = rest
    else:
        vn_ref, scratch = None, rest
    hist_buf, mixed_buf, ycat_buf, p_even, p_odd = scratch
    i = pl.program_id(0)
    seq_start = jnp.logical_or(i == 0, (i - 1) % tiles_per_seq == 0)

    @pl.when(i == 0)
    def _():
        p_odd[...] = jnp.zeros(p_odd.shape, _F32)
        hist_buf[...] = jnp.zeros(hist_buf.shape, _F32)

    step = functools.partial(
        _layer_step, xf_ref, xb_ref, prev_ref, ng_ref, win_ref, cw_ref, lng_ref, lnb_ref,
        sgw_ref, sgb_ref, gcv_ref, gsg_ref, wout_ref, fing_ref, y_ref, cs_ref, vn_ref,
        hist_buf, mixed_buf, ycat_buf, seq_start=seq_start, **static)

    @pl.when(i % 2 == 0)
    def _():
        step(p_front=p_even, p_back=p_odd)

    @pl.when(i % 2 == 1)
    def _():
        step(p_front=p_odd, p_back=p_even)


def _resident(shape):
    return pl.BlockSpec(shape, lambda i: (0,) * len(shape), pipeline_mode=pl.Buffered(1))


def _layer(x, prev, lw, *, nseg, seglen, ch, tiles_per_seq, final, emit_vn, name):
    n_rows = x.shape[0]
    rows = nseg * seglen
    n_tiles = n_rows // rows
    assert n_tiles * rows == n_rows and (nseg == 1 or tiles_per_seq == 1)
    assert prev.shape[0] * tiles_per_seq == n_tiles
    conv_slab, gate_slab = rows // CONV_SLABS, rows // GATE_SLABS
    assert seglen % conv_slab == 0 and gate_slab % ch == 0 and rows % ch == 0
    mixed_tile = lambda i: jnp.maximum(i - 1, 0)
    front_spec = pl.BlockSpec((rows, D_MODEL), lambda i: (jnp.minimum(i, n_tiles - 1), 0))
    back_spec = pl.BlockSpec((rows, D_MODEL), lambda i: (mixed_tile(i), 0))
    state_spec = pl.BlockSpec((None, nseg, CONV_W - 1, E_CONV),
                              lambda i: (mixed_tile(i) // tiles_per_seq, 0, 0, 0))
    out_shape = [jax.ShapeDtypeStruct(x.shape, _F32),
                 jax.ShapeDtypeStruct(prev.shape, _F32)]
    out_specs = [back_spec, state_spec]
    if emit_vn:
        out_shape.append(jax.ShapeDtypeStruct((n_rows, E_SGU), _F32))
        out_specs.append(back_spec)
    kern = functools.partial(_layer_kernel, nseg=nseg, seglen=seglen, ch=ch,
                             tiles_per_seq=tiles_per_seq, final=final, emit_vn=emit_vn)
    proj_buf = pltpu.VMEM((N_PROJ, rows, E_CONV), _F32)
    return pl.pallas_call(
        kern,
        grid=(n_tiles + 1,),
        in_specs=[
            front_spec, back_spec, state_spec,
            _resident((1, D_MODEL)),
            _resident((D_MODEL // 2, N_PROJ * E_CONV)),
            _resident((CONV_W, E_CONV)),
            _resident((1, E_SGU)), _resident((1, E_SGU)),
            _resident((G_SGU, SGU_CHUNK, SGU_CHUNK)),
            _resident((SGU_CHUNK, E_SGU)),
            _resident((1, E_CONV)), _resident((1, E_SGU)),
            _resident(((E_CONV + E_SGU) // 2, D_MODEL)),
            _resident((1, D_MODEL)),
        ],
        out_specs=out_specs,
        out_shape=out_shape,
        scratch_shapes=[pltpu.VMEM((nseg, SUBLANES, E_CONV), _F32),
                        pltpu.VMEM((rows, E_SGU), _F32),
                        pltpu.VMEM((rows, E_CONV + E_SGU), _BF16),
                        proj_buf, proj_buf],
        compiler_params=pltpu.CompilerParams(
            dimension_semantics=("arbitrary",),
            vmem_limit_bytes=VMEM_LIMIT_BYTES),
        name=name,
    )(x, x, prev, *lw)


def kernel(x_prompt, x_sample, state_conv, norm_g, w_in, conv_w, sg_ln_g, sg_ln_b, sg_w, sg_b,
           out_g_conv, out_g_sgu, w_out, final_g):
    depth = w_in.shape[0]
    batch, seq, _ = x_prompt.shape
    dec_batch, dec_seq, _ = x_sample.shape
    assert dec_seq <= CHUNK and TILE_ROWS % dec_seq == 0 and seq % TILE_ROWS == 0
    seqs_per_tile = TILE_ROWS // dec_seq
    dec_groups = dec_batch // seqs_per_tile

    def packed(w):
        pairs = w.astype(_BF16).reshape(w.shape[0] // 2, 2, w.shape[1])
        return lax.bitcast_convert_type(jnp.swapaxes(pairs, 1, 2), jnp.uint32)

    def layer_weights(l):
        row = lambda a: a[l].reshape(1, -1)
        bias = jnp.repeat(sg_b[l].T, SGU_HEAD, axis=1)
        return (row(norm_g), packed(w_in[l]), conv_w[l], row(sg_ln_g), row(sg_ln_b),
                sg_w[l], bias, row(out_g_conv), row(out_g_sgu), packed(w_out[l]),
                final_g.reshape(1, -1))

    xp = x_prompt.reshape(batch * seq, D_MODEL)
    xs = x_sample.reshape(dec_batch * dec_seq, D_MODEL)
    zero_prev = jnp.zeros((batch, 1, CONV_W - 1, E_CONV), _F32)
    conv_p, conv_s, v_s = [], [], []
    for l in range(depth):
        lw = layer_weights(l)
        final = l == depth - 1
        xp, cs = _layer(xp, zero_prev, lw, nseg=1, seglen=TILE_ROWS, ch=SGU_CHUNK,
                        tiles_per_seq=seq // TILE_ROWS, final=final, emit_vn=False,
                        name=f"prompt_layer{l}")
        conv_p.append(cs.reshape(batch, CONV_W - 1, E_CONV))
        prev = state_conv[l].reshape(dec_groups, seqs_per_tile, CONV_W - 1, E_CONV)
        xs, cs, vn = _layer(xs, prev, lw, nseg=seqs_per_tile, seglen=dec_seq, ch=dec_seq,
                            tiles_per_seq=1, final=final, emit_vn=True,
                            name=f"sample_layer{l}")
        conv_s.append(cs.reshape(dec_batch, CONV_W - 1, E_CONV))
        v_s.append(vn.reshape(dec_batch, dec_seq, E_SGU))
    return (xp.reshape(x_prompt.shape), xs.reshape(x_sample.shape), jnp.stack(conv_p, axis=0),
            jnp.stack(conv_s, axis=0), jnp.stack(v_s, axis=0))
```

```python
import functools

import jax
import jax.numpy as jnp
from jax import lax
from jax.experimental import pallas as pl
from jax.experimental.pallas import tpu as pltpu

D_MODEL = 1024
E_CONV = 1024
E_SGU = 1024
G_SGU = 8
SGU_HEAD = E_SGU // G_SGU
SGU_CHUNK = 128
CHUNK = 64
CONV_W = 3
N_PROJ = 7
EPS = 1e-6

SUBLANES = 8
BF16_ROWS = 16
TILE_ROWS = 256
CONV_SLABS = 4
GATE_SLABS = 2
VMEM_LIMIT_BYTES = 56 * 1024 * 1024

_F32 = jnp.float32
_BF16 = jnp.bfloat16


def _rms(v, g):
    return v * lax.rsqrt(jnp.mean(v * v, axis=-1, keepdims=True) + EPS) * g


def _after(v, done):
    bits = pltpu.bitcast(done, jnp.int32)
    parts = [bits[j:j + SUBLANES] for j in range(0, bits.shape[0], SUBLANES)]
    while len(parts) > 1:
        parts = [a | b for a, b in zip(parts[0::2], parts[1::2])] + parts[len(parts) & ~1:]
    zero = lax.shift_right_logical(lax.shift_right_logical(parts[0], 16), 16).astype(_F32)
    return v + jnp.concatenate([zero] * (v.shape[0] // SUBLANES), axis=0)


def _layer_step(xf_ref, xb_ref, prev_ref, ng_ref, win_ref, cw_ref, lng_ref, lnb_ref,
                sgw_ref, sgb_ref, gcv_ref, gsg_ref, wout_ref, fing_ref,
                y_ref, cs_ref, vn_ref, hist_buf, mixed_buf, ycat_buf, p_front, p_back,
                seq_start, *, nseg, seglen, ch, final):
    rows = nseg * seglen
    hist = SUBLANES - (CONV_W - 1)
    nch = rows // ch

    xn = _rms(xf_ref[...], ng_ref[...])
    xn_head = xn[0:BF16_ROWS]
    xn_tail = xn[BF16_ROWS:].astype(_BF16)

    def project(k, done=None):
        head = xn_head if done is None else _after(xn_head, done)
        lhs = jnp.concatenate([head.astype(_BF16), xn_tail], axis=0)
        w = pltpu.bitcast(win_ref[:, k * E_CONV:(k + 1) * E_CONV], _BF16)
        res = jnp.dot(lhs, w, preferred_element_type=_F32)
        p_front[k] = res
        return res[rows - SUBLANES:rows]

    def conv_slab(j, start):
        n = rows // CONV_SLABS
        r0 = j * n
        s = r0 // seglen
        h, bg, cg, za = (p_back[k, r0:r0 + n, :] for k in range(4))
        xc = cg * _after(h, start)
        ext = jnp.concatenate([hist_buf[s], xc], axis=0)
        xm1 = pltpu.roll(ext, 1, axis=0)[SUBLANES:]
        xm2 = pltpu.roll(ext, 2, axis=0)[SUBLANES:]
        hist_buf[s] = xc[n - SUBLANES:n]
        cw = cw_ref[...]
        yc = cw[0:1] * xm2 + cw[1:2] * xm1 + cw[2:3] * xc
        ya = (_rms(bg * yc, gcv_ref[...]) * jax.nn.silu(za)).astype(_BF16)
        ycat_buf[r0:r0 + n, 0:E_CONV] = ya
        return ya

    def sgu_mix():
        v = p_back[5]
        mu = jnp.mean(v, axis=-1, keepdims=True)
        vc = v - mu
        var = jnp.mean(vc * vc, axis=-1, keepdims=True)
        vn = vc * lax.rsqrt(var + EPS) * lng_ref[...] + lnb_ref[...]
        if vn_ref is not None:
            vn_ref[...] = vn
        vnb = vn.astype(_BF16)
        blk_t = lax.broadcasted_iota(jnp.int32, (ch, ch), 0) // CHUNK
        blk_s = lax.broadcasted_iota(jnp.int32, (ch, ch), 1) // CHUNK
        mask = (blk_t >= blk_s).astype(_F32)
        for g in range(G_SGU):
            wg = (sgw_ref[g, 0:ch, 0:ch] * mask).astype(_BF16)
            rhs = jnp.concatenate(
                [vnb[c * ch:(c + 1) * ch, g * SGU_HEAD:(g + 1) * SGU_HEAD]
                 for c in range(nch)], axis=1)
            m = jnp.dot(wg, rhs, preferred_element_type=_F32)
            for c in range(nch):
                mixed_buf[c * ch:(c + 1) * ch, g * SGU_HEAD:(g + 1) * SGU_HEAD] = (
                    m[:, c * SGU_HEAD:(c + 1) * SGU_HEAD])

    def gate_slab(j, start):
        n = rows // GATE_SLABS
        r0 = j * n
        u, zb = p_back[4, r0:r0 + n, :], p_back[6, r0:r0 + n, :]
        u = _after(u, start)
        bias = jnp.concatenate([sgb_ref[0:ch, :]] * (n // ch), axis=0)
        mixed = mixed_buf[r0:r0 + n, :] + bias
        yb = (_rms(u * mixed, gsg_ref[...]) * jax.nn.silu(zb)).astype(_BF16)
        ycat_buf[r0:r0 + n, E_CONV:E_CONV + E_SGU] = yb
        return yb

    carried = hist_buf[:, hist:SUBLANES, :]
    hist_buf[:, hist:SUBLANES, :] = jnp.where(seq_start, prev_ref[...], carried)

    tails = [project(0)]
    sgu_mix()
    tails.append(project(1))
    for j in range(CONV_SLABS):
        tails.append(project(2 + j, done=conv_slab(j, tails[j])))
    cs_ref[...] = hist_buf[:, hist:SUBLANES, :]
    tails.append(project(6, done=gate_slab(0, tails[4])))
    gate_slab(1, tails[5])

    out = xb_ref[...] + jnp.dot(ycat_buf[...], pltpu.bitcast(wout_ref[...], _BF16),
                               preferred_element_type=_F32)
    if final:
        out = _rms(out, fing_ref[...])
    y_ref[...] = out


def _layer_kernel(xf_ref, xb_ref, prev_ref, ng_ref, win_ref, cw_ref, lng_ref, lnb_ref,
                  sgw_ref, sgb_ref, gcv_ref, gsg_ref, wout_ref, fing_ref,
                  y_ref, cs_ref, *rest, tiles_per_seq, emit_vn, **static):
    if emit_vn:
        vn_ref, *scratch = rest
    else:
        vn_ref, scratch = None, rest
    hist_buf, mixed_buf, ycat_buf, p_even, p_odd = scratch
    i = pl.program_id(0)
    seq_start = jnp.logical_or(i == 0, (i - 1) % tiles_per_seq == 0)

    @pl.when(i == 0)
    def _():
        p_odd[...] = jnp.zeros(p_odd.shape, _F32)
        hist_buf[...] = jnp.zeros(hist_buf.shape, _F32)

    step = functools.partial(
        _layer_step, xf_ref, xb_ref, prev_ref, ng_ref, win_ref, cw_ref, lng_ref, lnb_ref,
        sgw_ref, sgb_ref, gcv_ref, gsg_ref, wout_ref, fing_ref, y_ref, cs_ref, vn_ref,
        hist_buf, mixed_buf, ycat_buf, seq_start=seq_start, **static)

    @pl.when(i % 2 == 0)
    def _():
        step(p_front=p_even, p_back=p_odd)

    @pl.when(i % 2 == 1)
    def _():
        step(p_front=p_odd, p_back=p_even)


def _resident(shape):
    return pl.BlockSpec(shape, lambda i: (0,) * len(shape), pipeline_mode=pl.Buffered(1))


def _layer(x, prev, lw, *, nseg, seglen, ch, tiles_per_seq, final, emit_vn, name):
    n_rows = x.shape[0]
    rows = nseg * seglen
    n_tiles = n_rows // rows
    assert n_tiles * rows == n_rows and (nseg == 1 or tiles_per_seq == 1)
    assert prev.shape[0] * tiles_per_seq == n_tiles
    conv_slab, gate_slab = rows // CONV_SLABS, rows // GATE_SLABS
    assert seglen % conv_slab == 0 and gate_slab % ch == 0 and rows % ch == 0
    mixed_tile = lambda i: jnp.maximum(i - 1, 0)
    front_spec = pl.BlockSpec((rows, D_MODEL), lambda i: (jnp.minimum(i, n_tiles - 1), 0))
    back_spec = pl.BlockSpec((rows, D_MODEL), lambda i: (mixed_tile(i), 0))
    state_spec = pl.BlockSpec((None, nseg, CONV_W - 1, E_CONV),
                              lambda i: (mixed_tile(i) // tiles_per_seq, 0, 0, 0))
    out_shape = [jax.ShapeDtypeStruct(x.shape, _F32),
                 jax.ShapeDtypeStruct(prev.shape, _F32)]
    out_specs = [back_spec, state_spec]
    if emit_vn:
        out_shape.append(jax.ShapeDtypeStruct((n_rows, E_SGU), _F32))
        out_specs.append(back_spec)
    kern = functools.partial(_layer_kernel, nseg=nseg, seglen=seglen, ch=ch,
                             tiles_per_seq=tiles_per_seq, final=final, emit_vn=emit_vn)
    proj_buf = pltpu.VMEM((N_PROJ, rows, E_CONV), _F32)
    return pl.pallas_call(
        kern,
        grid=(n_tiles + 1,),
        in_specs=[
            front_spec, back_spec, state_spec,
            _resident((1, D_MODEL)),
            _resident((D_MODEL // 2, N_PROJ * E_CONV)),
            _resident((CONV_W, E_CONV)),
            _resident((1, E_SGU)), _resident((1, E_SGU)),
            _resident((G_SGU, SGU_CHUNK, SGU_CHUNK)),
            _resident((SGU_CHUNK, E_SGU)),
            _resident((1, E_CONV)), _resident((1, E_SGU)),
            _resident(((E_CONV + E_SGU) // 2, D_MODEL)),
            _resident((1, D_MODEL)),
        ],
        out_specs=out_specs,
        out_shape=out_shape,
        scratch_shapes=[pltpu.VMEM((nseg, SUBLANES, E_CONV), _F32),
                        pltpu.VMEM((rows, E_SGU), _F32),
                        pltpu.VMEM((rows, E_CONV + E_SGU), _BF16),
                        proj_buf, proj_buf],
        compiler_params=pltpu.CompilerParams(
            dimension_semantics=("arbitrary",),
            vmem_limit_bytes=VMEM_LIMIT_BYTES),
        name=name,
    )(x, x, prev, *lw)


def kernel(x_prompt, x_sample, state_conv, norm_g, w_in, conv_w, sg_ln_g, sg_ln_b, sg_w, sg_b,
           out_g_conv, out_g_sgu, w_out, final_g):
    depth = w_in.shape[0]
    batch, seq, _ = x_prompt.shape
    dec_batch, dec_seq, _ = x_sample.shape
    assert dec_seq <= CHUNK and TILE_ROWS % dec_seq == 0 and seq % TILE_ROWS == 0
    seqs_per_tile = TILE_ROWS // dec_seq
    dec_groups = dec_batch // seqs_per_tile

    def packed(w):
        half = lambda v: lax.bitcast_convert_type(v.astype(_BF16), jnp.uint16).astype(jnp.uint32)
        return half(w[:, 0::2, :]) | (half(w[:, 1::2, :]) << 16)

    w_in_p, w_out_p = packed(w_in), packed(w_out)

    def layer_weights(l):
        row = lambda a: a[l].reshape(1, -1)
        bias = jnp.repeat(sg_b[l].T, SGU_HEAD, axis=1)
        return (row(norm_g), w_in_p[l], conv_w[l], row(sg_ln_g), row(sg_ln_b),
                sg_w[l], bias, row(out_g_conv), row(out_g_sgu), w_out_p[l],
                final_g.reshape(1, -1))

    xp = x_prompt.reshape(batch * seq, D_MODEL)
    xs = x_sample.reshape(dec_batch * dec_seq, D_MODEL)
    zero_prev = jnp.zeros((batch, 1, CONV_W - 1, E_CONV), _F32)
    conv_p, conv_s, v_s = [], [], []
    for l in range(depth):
        lw = layer_weights(l)
        final = l == depth - 1
        xp, cs = _layer(xp, zero_prev, lw, nseg=1, seglen=TILE_ROWS, ch=SGU_CHUNK,
                        tiles_per_seq=seq // TILE_ROWS, final=final, emit_vn=False,
                        name=f"prompt_layer{l}")
        conv_p.append(cs.reshape(batch, CONV_W - 1, E_CONV))
        prev = state_conv[l].reshape(dec_groups, seqs_per_tile, CONV_W - 1, E_CONV)
        xs, cs, vn = _layer(xs, prev, lw, nseg=seqs_per_tile, seglen=dec_seq, ch=dec_seq,
                            tiles_per_seq=1, final=final, emit_vn=True,
                            name=f"sample_layer{l}")
        conv_s.append(cs.reshape(dec_batch, CONV_W - 1, E_CONV))
        v_s.append(vn.reshape(dec_batch, dec_seq, E_SGU))
    return (xp.reshape(x_prompt.shape), xs.reshape(x_sample.shape), jnp.stack(conv_p, axis=0),
            jnp.stack(conv_s, axis=0), jnp.stack(v_s, axis=0))
```

```python
import functools

import jax
import jax.numpy as jnp
from jax import lax
from jax.experimental import pallas as pl
from jax.experimental.pallas import tpu as pltpu

D_MODEL = 1024
E_CONV = 1024
E_SGU = 1024
G_SGU = 8
SGU_HEAD = E_SGU // G_SGU
SGU_CHUNK = 128
CHUNK = 64
CONV_W = 3
N_PROJ = 7
EPS = 1e-6

SUBLANES = 8
BF16_ROWS = 16
TILE_ROWS = 256
CONV_SLABS = 4
GATE_SLABS = 2
VMEM_LIMIT_BYTES = 56 * 1024 * 1024
PACK_BLOCK_BYTES = 4 * 1024 * 1024

_F32 = jnp.float32
_BF16 = jnp.bfloat16


def _rms(v, g):
    return v * lax.rsqrt(jnp.mean(v * v, axis=-1, keepdims=True) + EPS) * g


def _after(v, done):
    bits = pltpu.bitcast(done, jnp.int32)
    parts = [bits[j:j + SUBLANES] for j in range(0, bits.shape[0], SUBLANES)]
    while len(parts) > 1:
        parts = [a | b for a, b in zip(parts[0::2], parts[1::2])] + parts[len(parts) & ~1:]
    zero = lax.shift_right_logical(lax.shift_right_logical(parts[0], 16), 16).astype(_F32)
    return v + jnp.concatenate([zero] * (v.shape[0] // SUBLANES), axis=0)


def _layer_step(xf_ref, xb_ref, prev_ref, ng_ref, win_ref, cw_ref, lng_ref, lnb_ref,
                sgw_ref, sgb_ref, gcv_ref, gsg_ref, wout_ref, fing_ref,
                y_ref, cs_ref, vn_ref, hist_buf, mixed_buf, ycat_buf, p_front, p_back,
                seq_start, *, nseg, seglen, ch, final):
    rows = nseg * seglen
    hist = SUBLANES - (CONV_W - 1)
    nch = rows // ch

    xn = _rms(xf_ref[...], ng_ref[...])
    xn_head = xn[0:BF16_ROWS]
    xn_tail = xn[BF16_ROWS:].astype(_BF16)

    def project(k, done=None):
        head = xn_head if done is None else _after(xn_head, done)
        lhs = jnp.concatenate([head.astype(_BF16), xn_tail], axis=0)
        w = pltpu.bitcast(win_ref[:, k * E_CONV:(k + 1) * E_CONV], _BF16)
        res = jnp.dot(lhs, w, preferred_element_type=_F32)
        p_front[k] = res
        return res[rows - SUBLANES:rows]

    def conv_slab(j, start):
        n = rows // CONV_SLABS
        r0 = j * n
        s = r0 // seglen
        h, bg, cg, za = (p_back[k, r0:r0 + n, :] for k in range(4))
        xc = cg * _after(h, start)
        ext = jnp.concatenate([hist_buf[s], xc], axis=0)
        xm1 = pltpu.roll(ext, 1, axis=0)[SUBLANES:]
        xm2 = pltpu.roll(ext, 2, axis=0)[SUBLANES:]
        hist_buf[s] = xc[n - SUBLANES:n]
        cw = cw_ref[...]
        yc = cw[0:1] * xm2 + cw[1:2] * xm1 + cw[2:3] * xc
        ya = (_rms(bg * yc, gcv_ref[...]) * jax.nn.silu(za)).astype(_BF16)
        ycat_buf[r0:r0 + n, 0:E_CONV] = ya
        return ya

    def sgu_mix():
        v = p_back[5]
        mu = jnp.mean(v, axis=-1, keepdims=True)
        vc = v - mu
        var = jnp.mean(vc * vc, axis=-1, keepdims=True)
        vn = vc * lax.rsqrt(var + EPS) * lng_ref[...] + lnb_ref[...]
        if vn_ref is not None:
            vn_ref[...] = vn
        vnb = vn.astype(_BF16)
        blk_t = lax.broadcasted_iota(jnp.int32, (ch, ch), 0) // CHUNK
        blk_s = lax.broadcasted_iota(jnp.int32, (ch, ch), 1) // CHUNK
        mask = (blk_t >= blk_s).astype(_F32)
        for g in range(G_SGU):
            wg = (sgw_ref[g, 0:ch, 0:ch] * mask).astype(_BF16)
            rhs = jnp.concatenate(
                [vnb[c * ch:(c + 1) * ch, g * SGU_HEAD:(g + 1) * SGU_HEAD]
                 for c in range(nch)], axis=1)
            m = jnp.dot(wg, rhs, preferred_element_type=_F32)
            for c in range(nch):
                mixed_buf[c * ch:(c + 1) * ch, g * SGU_HEAD:(g + 1) * SGU_HEAD] = (
                    m[:, c * SGU_HEAD:(c + 1) * SGU_HEAD])

    def gate_slab(j, start):
        n = rows // GATE_SLABS
        r0 = j * n
        u, zb = p_back[4, r0:r0 + n, :], p_back[6, r0:r0 + n, :]
        u = _after(u, start)
        bias = jnp.concatenate([sgb_ref[0:ch, :]] * (n // ch), axis=0)
        mixed = mixed_buf[r0:r0 + n, :] + bias
        yb = (_rms(u * mixed, gsg_ref[...]) * jax.nn.silu(zb)).astype(_BF16)
        ycat_buf[r0:r0 + n, E_CONV:E_CONV + E_SGU] = yb
        return yb

    carried = hist_buf[:, hist:SUBLANES, :]
    hist_buf[:, hist:SUBLANES, :] = jnp.where(seq_start, prev_ref[...], carried)

    tails = [project(0)]
    sgu_mix()
    tails.append(project(1))
    for j in range(CONV_SLABS):
        tails.append(project(2 + j, done=conv_slab(j, tails[j])))
    cs_ref[...] = hist_buf[:, hist:SUBLANES, :]
    tails.append(project(6, done=gate_slab(0, tails[4])))
    gate_slab(1, tails[5])

    out = xb_ref[...] + jnp.dot(ycat_buf[...], pltpu.bitcast(wout_ref[...], _BF16),
                               preferred_element_type=_F32)
    if final:
        out = _rms(out, fing_ref[...])
    y_ref[...] = out


def _layer_kernel(xf_ref, xb_ref, prev_ref, ng_ref, win_ref, cw_ref, lng_ref, lnb_ref,
                  sgw_ref, sgb_ref, gcv_ref, gsg_ref, wout_ref, fing_ref,
                  y_ref, cs_ref, *rest, tiles_per_seq, emit_vn, **static):
    if emit_vn:
        vn_ref, *scratch = rest
    else:
        vn_ref, scratch = None, rest
    hist_buf, mixed_buf, ycat_buf, p_even, p_odd = scratch
    i = pl.program_id(0)
    seq_start = jnp.logical_or(i == 0, (i - 1) % tiles_per_seq == 0)

    @pl.when(i == 0)
    def _():
        p_odd[...] = jnp.zeros(p_odd.shape, _F32)
        hist_buf[...] = jnp.zeros(hist_buf.shape, _F32)

    step = functools.partial(
        _layer_step, xf_ref, xb_ref, prev_ref, ng_ref, win_ref, cw_ref, lng_ref, lnb_ref,
        sgw_ref, sgb_ref, gcv_ref, gsg_ref, wout_ref, fing_ref, y_ref, cs_ref, vn_ref,
        hist_buf, mixed_buf, ycat_buf, seq_start=seq_start, **static)

    @pl.when(i % 2 == 0)
    def _():
        step(p_front=p_even, p_back=p_odd)

    @pl.when(i % 2 == 1)
    def _():
        step(p_front=p_odd, p_back=p_even)


def _resident(shape):
    return pl.BlockSpec(shape, lambda i: (0,) * len(shape), pipeline_mode=pl.Buffered(1))


def _resident_layer(shape, l):
    return pl.BlockSpec((None, *shape), lambda i: (l,) + (0,) * len(shape),
                        pipeline_mode=pl.Buffered(1))


def _pack_kernel(w_ref, o_ref):
    o_ref[...] = pltpu.bitcast(w_ref[...].astype(_BF16), jnp.uint32)


def _pack_rows(w):
    depth, k, n = w.shape
    bk = k
    while bk * n * w.dtype.itemsize > PACK_BLOCK_BYTES and bk % (4 * BF16_ROWS) == 0:
        bk //= 2
    return pl.pallas_call(
        _pack_kernel,
        grid=(depth, k // bk),
        in_specs=[pl.BlockSpec((None, bk, n), lambda l, j: (l, j, 0))],
        out_specs=pl.BlockSpec((None, bk // 2, n), lambda l, j: (l, j, 0)),
        out_shape=jax.ShapeDtypeStruct((depth, k // 2, n), jnp.uint32),
        name="pack_rows",
    )(w)


def _layer(x, prev, lw, *, layer, nseg, seglen, ch, tiles_per_seq, final, emit_vn, name):
    n_rows = x.shape[0]
    rows = nseg * seglen
    n_tiles = n_rows // rows
    assert n_tiles * rows == n_rows and (nseg == 1 or tiles_per_seq == 1)
    assert prev.shape[0] * tiles_per_seq == n_tiles
    conv_slab, gate_slab = rows // CONV_SLABS, rows // GATE_SLABS
    assert seglen % conv_slab == 0 and gate_slab % ch == 0 and rows % ch == 0
    mixed_tile = lambda i: jnp.maximum(i - 1, 0)
    front_spec = pl.BlockSpec((rows, D_MODEL), lambda i: (jnp.minimum(i, n_tiles - 1), 0))
    back_spec = pl.BlockSpec((rows, D_MODEL), lambda i: (mixed_tile(i), 0))
    state_spec = pl.BlockSpec((None, nseg, CONV_W - 1, E_CONV),
                              lambda i: (mixed_tile(i) // tiles_per_seq, 0, 0, 0))
    out_shape = [jax.ShapeDtypeStruct(x.shape, _F32),
                 jax.ShapeDtypeStruct(prev.shape, _F32)]
    out_specs = [back_spec, state_spec]
    if emit_vn:
        out_shape.append(jax.ShapeDtypeStruct((n_rows, E_SGU), _F32))
        out_specs.append(back_spec)
    kern = functools.partial(_layer_kernel, nseg=nseg, seglen=seglen, ch=ch,
                             tiles_per_seq=tiles_per_seq, final=final, emit_vn=emit_vn)
    proj_buf = pltpu.VMEM((N_PROJ, rows, E_CONV), _F32)
    return pl.pallas_call(
        kern,
        grid=(n_tiles + 1,),
        in_specs=[
            front_spec, back_spec, state_spec,
            _resident((1, D_MODEL)),
            _resident_layer((D_MODEL // 2, N_PROJ * E_CONV), layer),
            _resident((CONV_W, E_CONV)),
            _resident((1, E_SGU)), _resident((1, E_SGU)),
            _resident((G_SGU, SGU_CHUNK, SGU_CHUNK)),
            _resident((SGU_CHUNK, E_SGU)),
            _resident((1, E_CONV)), _resident((1, E_SGU)),
            _resident_layer(((E_CONV + E_SGU) // 2, D_MODEL), layer),
            _resident((1, D_MODEL)),
        ],
        out_specs=out_specs,
        out_shape=out_shape,
        scratch_shapes=[pltpu.VMEM((nseg, SUBLANES, E_CONV), _F32),
                        pltpu.VMEM((rows, E_SGU), _F32),
                        pltpu.VMEM((rows, E_CONV + E_SGU), _BF16),
                        proj_buf, proj_buf],
        compiler_params=pltpu.CompilerParams(
            dimension_semantics=("arbitrary",),
            vmem_limit_bytes=VMEM_LIMIT_BYTES),
        name=name,
    )(x, x, prev, *lw)


def kernel(x_prompt, x_sample, state_conv, norm_g, w_in, conv_w, sg_ln_g, sg_ln_b, sg_w, sg_b,
           out_g_conv, out_g_sgu, w_out, final_g):
    depth = w_in.shape[0]
    batch, seq, _ = x_prompt.shape
    dec_batch, dec_seq, _ = x_sample.shape
    assert dec_seq <= CHUNK and TILE_ROWS % dec_seq == 0 and seq % TILE_ROWS == 0
    seqs_per_tile = TILE_ROWS // dec_seq
    dec_groups = dec_batch // seqs_per_tile

    w_in_p, w_out_p = _pack_rows(w_in), _pack_rows(w_out)

    def layer_weights(l):
        row = lambda a: a[l].reshape(1, -1)
        bias = jnp.repeat(sg_b[l].T, SGU_HEAD, axis=1)
        return (row(norm_g), w_in_p, conv_w[l], row(sg_ln_g), row(sg_ln_b),
                sg_w[l], bias, row(out_g_conv), row(out_g_sgu), w_out_p,
                final_g.reshape(1, -1))

    xp = x_prompt.reshape(batch * seq, D_MODEL)
    xs = x_sample.reshape(dec_batch * dec_seq, D_MODEL)
    zero_prev = jnp.zeros((batch, 1, CONV_W - 1, E_CONV), _F32)
    conv_p, conv_s, v_s = [], [], []
    for l in range(depth):
        lw = layer_weights(l)
        final = l == depth - 1
        xp, cs = _layer(xp, zero_prev, lw, layer=l, nseg=1, seglen=TILE_ROWS, ch=SGU_CHUNK,
                        tiles_per_seq=seq // TILE_ROWS, final=final, emit_vn=False,
                        name=f"prompt_layer{l}")
        conv_p.append(cs.reshape(batch, CONV_W - 1, E_CONV))
        prev = state_conv[l].reshape(dec_groups, seqs_per_tile, CONV_W - 1, E_CONV)
        xs, cs, vn = _layer(xs, prev, lw, layer=l, nseg=seqs_per_tile, seglen=dec_seq,
                            ch=dec_seq, tiles_per_seq=1, final=final, emit_vn=True,
                            name=f"sample_layer{l}")
        conv_s.append(cs.reshape(dec_batch, CONV_W - 1, E_CONV))
        v_s.append(vn.reshape(dec_batch, dec_seq, E_SGU))
    return (xp.reshape(x_prompt.shape), xs.reshape(x_sample.shape), jnp.stack(conv_p, axis=0),
            jnp.stack(conv_s, axis=0), jnp.stack(v_s, axis=0))
```

```python
import functools

import jax
import jax.numpy as jnp
from jax import lax
from jax.experimental import pallas as pl
from jax.experimental.pallas import tpu as pltpu

D_MODEL = 1024
E_CONV = 1024
E_SGU = 1024
G_SGU = 8
SGU_HEAD = E_SGU // G_SGU
SGU_CHUNK = 128
CHUNK = 64
CONV_W = 3
N_PROJ = 7
EPS = 1e-6

SUBLANES = 8
BF16_ROWS = 16
TILE_ROWS = 256
CONV_SLABS = 4
GATE_SLABS = 2
VMEM_LIMIT_BYTES = 56 * 1024 * 1024
PACK_BLOCK_BYTES = 4 * 1024 * 1024

_F32 = jnp.float32
_BF16 = jnp.bfloat16


def _rms(v, g):
    return v * lax.rsqrt(jnp.mean(v * v, axis=-1, keepdims=True) + EPS) * g


def _after(v, done):
    bits = pltpu.bitcast(done, jnp.int32)
    parts = [bits[j:j + SUBLANES] for j in range(0, bits.shape[0], SUBLANES)]
    while len(parts) > 1:
        parts = [a | b for a, b in zip(parts[0::2], parts[1::2])] + parts[len(parts) & ~1:]
    zero = lax.shift_right_logical(lax.shift_right_logical(parts[0], 16), 16).astype(_F32)
    return v + jnp.concatenate([zero] * (v.shape[0] // SUBLANES), axis=0)


def _layer_step(xf_ref, xb_ref, prev_ref, ng_ref, win_ref, cw_ref, lng_ref, lnb_ref,
                sgw_ref, sgb_ref, gcv_ref, gsg_ref, wout_ref, fing_ref,
                y_ref, cs_ref, vn_ref, hist_buf, mixed_buf, ycat_buf, wsg_buf, p_front, p_back,
                seq_start, *, nseg, seglen, ch, final):
    rows = nseg * seglen
    hist = SUBLANES - (CONV_W - 1)
    nch = rows // ch

    xn = _rms(xf_ref[...], ng_ref[...])
    xn_head = xn[0:BF16_ROWS]
    xn_tail = xn[BF16_ROWS:].astype(_BF16)

    def project(k, done=None):
        head = xn_head if done is None else _after(xn_head, done)
        lhs = jnp.concatenate([head.astype(_BF16), xn_tail], axis=0)
        w = pltpu.bitcast(win_ref[:, k * E_CONV:(k + 1) * E_CONV], _BF16)
        res = jnp.dot(lhs, w, preferred_element_type=_F32)
        p_front[k] = res
        return res[rows - SUBLANES:rows]

    def conv_slab(j, start):
        n = rows // CONV_SLABS
        r0 = j * n
        s = r0 // seglen
        h, bg, cg, za = (p_back[k, r0:r0 + n, :] for k in range(4))
        xc = cg * _after(h, start)
        ext = jnp.concatenate([hist_buf[s], xc], axis=0)
        xm1 = pltpu.roll(ext, 1, axis=0)[SUBLANES:]
        xm2 = pltpu.roll(ext, 2, axis=0)[SUBLANES:]
        hist_buf[s] = xc[n - SUBLANES:n]
        cw = cw_ref[...]
        yc = cw[0:1] * xm2 + cw[1:2] * xm1 + cw[2:3] * xc
        ya = (_rms(bg * yc, gcv_ref[...]) * jax.nn.silu(za)).astype(_BF16)
        ycat_buf[r0:r0 + n, 0:E_CONV] = ya
        return ya

    def sgu_mix():
        v = p_back[5]
        mu = jnp.mean(v, axis=-1, keepdims=True)
        vc = v - mu
        var = jnp.mean(vc * vc, axis=-1, keepdims=True)
        vn = vc * lax.rsqrt(var + EPS) * lng_ref[...] + lnb_ref[...]
        if vn_ref is not None:
            vn_ref[...] = vn
        vnb = vn.astype(_BF16)
        for g in range(G_SGU):
            wg = wsg_buf[g, 0:ch, 0:ch]
            rhs = jnp.concatenate(
                [vnb[c * ch:(c + 1) * ch, g * SGU_HEAD:(g + 1) * SGU_HEAD]
                 for c in range(nch)], axis=1)
            m = jnp.dot(wg, rhs, preferred_element_type=_F32)
            for c in range(nch):
                mixed_buf[c * ch:(c + 1) * ch, g * SGU_HEAD:(g + 1) * SGU_HEAD] = (
                    m[:, c * SGU_HEAD:(c + 1) * SGU_HEAD])

    def gate_slab(j, start):
        n = rows // GATE_SLABS
        r0 = j * n
        u, zb = p_back[4, r0:r0 + n, :], p_back[6, r0:r0 + n, :]
        u = _after(u, start)
        bias = jnp.concatenate([sgb_ref[0:ch, :]] * (n // ch), axis=0)
        mixed = mixed_buf[r0:r0 + n, :] + bias
        yb = (_rms(u * mixed, gsg_ref[...]) * jax.nn.silu(zb)).astype(_BF16)
        ycat_buf[r0:r0 + n, E_CONV:E_CONV + E_SGU] = yb
        return yb

    carried = hist_buf[:, hist:SUBLANES, :]
    hist_buf[:, hist:SUBLANES, :] = jnp.where(seq_start, prev_ref[...], carried)

    tails = [project(0)]
    sgu_mix()
    tails.append(project(1))
    for j in range(CONV_SLABS):
        tails.append(project(2 + j, done=conv_slab(j, tails[j])))
    cs_ref[...] = hist_buf[:, hist:SUBLANES, :]
    tails.append(project(6, done=gate_slab(0, tails[4])))
    gate_slab(1, tails[5])

    out = xb_ref[...] + jnp.dot(ycat_buf[...], pltpu.bitcast(wout_ref[...], _BF16),
                               preferred_element_type=_F32)
    if final:
        out = _rms(out, fing_ref[...])
    y_ref[...] = out


def _layer_kernel(xf_ref, xb_ref, prev_ref, ng_ref, win_ref, cw_ref, lng_ref, lnb_ref,
                  sgw_ref, sgb_ref, gcv_ref, gsg_ref, wout_ref, fing_ref,
                  y_ref, cs_ref, *rest, tiles_per_seq, emit_vn, **static):
    if emit_vn:
        vn_ref, *scratch = rest
    else:
        vn_ref, scratch = None, rest
    hist_buf, mixed_buf, ycat_buf, wsg_buf, p_even, p_odd = scratch
    i = pl.program_id(0)
    seq_start = jnp.logical_or(i == 0, (i - 1) % tiles_per_seq == 0)

    @pl.when(i == 0)
    def _():
        p_odd[...] = jnp.zeros(p_odd.shape, _F32)
        hist_buf[...] = jnp.zeros(hist_buf.shape, _F32)
        blk_t = lax.broadcasted_iota(jnp.int32, (SGU_CHUNK, SGU_CHUNK), 0) // CHUNK
        blk_s = lax.broadcasted_iota(jnp.int32, (SGU_CHUNK, SGU_CHUNK), 1) // CHUNK
        mask = (blk_t >= blk_s).astype(_F32)
        for g in range(G_SGU):
            wsg_buf[g] = (sgw_ref[g] * mask).astype(_BF16)

    step = functools.partial(
        _layer_step, xf_ref, xb_ref, prev_ref, ng_ref, win_ref, cw_ref, lng_ref, lnb_ref,
        sgw_ref, sgb_ref, gcv_ref, gsg_ref, wout_ref, fing_ref, y_ref, cs_ref, vn_ref,
        hist_buf, mixed_buf, ycat_buf, wsg_buf, seq_start=seq_start, **static)

    @pl.when(i % 2 == 0)
    def _():
        step(p_front=p_even, p_back=p_odd)

    @pl.when(i % 2 == 1)
    def _():
        step(p_front=p_odd, p_back=p_even)


def _resident(shape):
    return pl.BlockSpec(shape, lambda i: (0,) * len(shape), pipeline_mode=pl.Buffered(1))


def _resident_layer(shape, l):
    return pl.BlockSpec((None, *shape), lambda i: (l,) + (0,) * len(shape),
                        pipeline_mode=pl.Buffered(1))


def _pack_kernel(w_ref, o_ref):
    o_ref[...] = pltpu.bitcast(w_ref[...].astype(_BF16), jnp.uint32)


def _pack_rows(w):
    depth, k, n = w.shape
    bk = k
    while bk * n * w.dtype.itemsize > PACK_BLOCK_BYTES and bk % (4 * BF16_ROWS) == 0:
        bk //= 2
    return pl.pallas_call(
        _pack_kernel,
        grid=(depth, k // bk),
        in_specs=[pl.BlockSpec((None, bk, n), lambda l, j: (l, j, 0))],
        out_specs=pl.BlockSpec((None, bk // 2, n), lambda l, j: (l, j, 0)),
        out_shape=jax.ShapeDtypeStruct((depth, k // 2, n), jnp.uint32),
        name="pack_rows",
    )(w)


def _layer(x, prev, lw, *, layer, nseg, seglen, ch, tiles_per_seq, final, emit_vn, name):
    n_rows = x.shape[0]
    rows = nseg * seglen
    n_tiles = n_rows // rows
    assert n_tiles * rows == n_rows and (nseg == 1 or tiles_per_seq == 1)
    assert prev.shape[0] * tiles_per_seq == n_tiles
    conv_slab, gate_slab = rows // CONV_SLABS, rows // GATE_SLABS
    assert seglen % conv_slab == 0 and gate_slab % ch == 0 and rows % ch == 0
    mixed_tile = lambda i: jnp.maximum(i - 1, 0)
    front_spec = pl.BlockSpec((rows, D_MODEL), lambda i: (jnp.minimum(i, n_tiles - 1), 0))
    back_spec = pl.BlockSpec((rows, D_MODEL), lambda i: (mixed_tile(i), 0))
    state_spec = pl.BlockSpec((None, nseg, CONV_W - 1, E_CONV),
                              lambda i: (mixed_tile(i) // tiles_per_seq, 0, 0, 0))
    out_shape = [jax.ShapeDtypeStruct(x.shape, _F32),
                 jax.ShapeDtypeStruct(prev.shape, _F32)]
    out_specs = [back_spec, state_spec]
    if emit_vn:
        out_shape.append(jax.ShapeDtypeStruct((n_rows, E_SGU), _F32))
        out_specs.append(back_spec)
    kern = functools.partial(_layer_kernel, nseg=nseg, seglen=seglen, ch=ch,
                             tiles_per_seq=tiles_per_seq, final=final, emit_vn=emit_vn)
    proj_buf = pltpu.VMEM((N_PROJ, rows, E_CONV), _F32)
    return pl.pallas_call(
        kern,
        grid=(n_tiles + 1,),
        in_specs=[
            front_spec, back_spec, state_spec,
            _resident((1, D_MODEL)),
            _resident_layer((D_MODEL // 2, N_PROJ * E_CONV), layer),
            _resident((CONV_W, E_CONV)),
            _resident((1, E_SGU)), _resident((1, E_SGU)),
            _resident((G_SGU, SGU_CHUNK, SGU_CHUNK)),
            _resident((SGU_CHUNK, E_SGU)),
            _resident((1, E_CONV)), _resident((1, E_SGU)),
            _resident_layer(((E_CONV + E_SGU) // 2, D_MODEL), layer),
            _resident((1, D_MODEL)),
        ],
        out_specs=out_specs,
        out_shape=out_shape,
        scratch_shapes=[pltpu.VMEM((nseg, SUBLANES, E_CONV), _F32),
                        pltpu.VMEM((rows, E_SGU), _F32),
                        pltpu.VMEM((rows, E_CONV + E_SGU), _BF16),
                        pltpu.VMEM((G_SGU, SGU_CHUNK, SGU_CHUNK), _BF16),
                        proj_buf, proj_buf],
        compiler_params=pltpu.CompilerParams(
            dimension_semantics=("arbitrary",),
            vmem_limit_bytes=VMEM_LIMIT_BYTES),
        name=name,
    )(x, x, prev, *lw)


def kernel(x_prompt, x_sample, state_conv, norm_g, w_in, conv_w, sg_ln_g, sg_ln_b, sg_w, sg_b,
           out_g_conv, out_g_sgu, w_out, final_g):
    depth = w_in.shape[0]
    batch, seq, _ = x_prompt.shape
    dec_batch, dec_seq, _ = x_sample.shape
    assert dec_seq <= CHUNK and TILE_ROWS % dec_seq == 0 and seq % TILE_ROWS == 0
    seqs_per_tile = TILE_ROWS // dec_seq
    dec_groups = dec_batch // seqs_per_tile

    w_in_p, w_out_p = _pack_rows(w_in), _pack_rows(w_out)

    def layer_weights(l):
        row = lambda a: a[l].reshape(1, -1)
        bias = jnp.repeat(sg_b[l].T, SGU_HEAD, axis=1)
        return (row(norm_g), w_in_p, conv_w[l], row(sg_ln_g), row(sg_ln_b),
                sg_w[l], bias, row(out_g_conv), row(out_g_sgu), w_out_p,
                final_g.reshape(1, -1))

    xp = x_prompt.reshape(batch * seq, D_MODEL)
    xs = x_sample.reshape(dec_batch * dec_seq, D_MODEL)
    zero_prev = jnp.zeros((batch, 1, CONV_W - 1, E_CONV), _F32)
    conv_p, conv_s, v_s = [], [], []
    for l in range(depth):
        lw = layer_weights(l)
        final = l == depth - 1
        xp, cs = _layer(xp, zero_prev, lw, layer=l, nseg=1, seglen=TILE_ROWS, ch=SGU_CHUNK,
                        tiles_per_seq=seq // TILE_ROWS, final=final, emit_vn=False,
                        name=f"prompt_layer{l}")
        conv_p.append(cs.reshape(batch, CONV_W - 1, E_CONV))
        prev = state_conv[l].reshape(dec_groups, seqs_per_tile, CONV_W - 1, E_CONV)
        xs, cs, vn = _layer(xs, prev, lw, layer=l, nseg=seqs_per_tile, seglen=dec_seq,
                            ch=dec_seq, tiles_per_seq=1, final=final, emit_vn=True,
                            name=f"sample_layer{l}")
        conv_s.append(cs.reshape(dec_batch, CONV_W - 1, E_CONV))
        v_s.append(vn.reshape(dec_batch, dec_seq, E_SGU))
    return (xp.reshape(x_prompt.shape), xs.reshape(x_sample.shape), jnp.stack(conv_p, axis=0),
            jnp.stack(conv_s, axis=0), jnp.stack(v_s, axis=0))
```

```python
import functools

import jax
import jax.numpy as jnp
from jax import lax
from jax.experimental import pallas as pl
from jax.experimental.pallas import tpu as pltpu

D_MODEL = 1024
E_CONV = 1024
E_SGU = 1024
G_SGU = 8
SGU_HEAD = E_SGU // G_SGU
SGU_CHUNK = 128
CHUNK = 64
CONV_W = 3
N_PROJ = 7
EPS = 1e-6

SUBLANES = 8
BF16_ROWS = 16
TILE_ROWS = 256
CONV_SLABS = 4
GATE_SLABS = 2
VMEM_LIMIT_BYTES = 56 * 1024 * 1024
PACK_BLOCK_BYTES = 4 * 1024 * 1024

_F32 = jnp.float32
_BF16 = jnp.bfloat16


def _row_scale(v):
    return lax.rsqrt(jnp.mean(v * v, axis=-1, keepdims=True) + EPS)


def _rms(v, g):
    return v * _row_scale(v) * g


def _zero_of(done):
    bits = pltpu.bitcast(done, jnp.int32)
    parts = [bits[j:j + SUBLANES] for j in range(0, bits.shape[0], SUBLANES)]
    while len(parts) > 1:
        parts = [a | b for a, b in zip(parts[0::2], parts[1::2])] + parts[len(parts) & ~1:]
    return lax.shift_right_logical(lax.shift_right_logical(parts[0], 16), 16).astype(_F32)


def _after(v, done):
    zero = sum(_zero_of(d) for d in (done if isinstance(done, (list, tuple)) else [done]))
    zero = jnp.broadcast_to(zero, (SUBLANES, v.shape[1]))
    return v + jnp.concatenate([zero] * (v.shape[0] // SUBLANES), axis=0)


def _layer_step(xf_ref, xnext_ref, xb_ref, prev_ref, ng_ref, win_ref, cw_ref, lng_ref, lnb_ref,
                sgw_ref, sgb_ref, gcv_ref, gsg_ref, wout_ref, fing_ref,
                y_ref, cs_ref, vn_ref, hist_buf, mixed_buf, ycat_buf, wsg_buf, scale_buf,
                p_front, p_back, seq_start, *, nseg, seglen, ch, final):
    rows = nseg * seglen
    hist = SUBLANES - (CONV_W - 1)
    nch = rows // ch

    xn = xf_ref[...] * scale_buf[...] * ng_ref[...]
    xn_head = xn[0:BF16_ROWS]
    xn_tail = xn[BF16_ROWS:].astype(_BF16)

    def project(k, done=None):
        head = xn_head if done is None else _after(xn_head, done)
        lhs = jnp.concatenate([head.astype(_BF16), xn_tail], axis=0)
        w = pltpu.bitcast(win_ref[:, k * E_CONV:(k + 1) * E_CONV], _BF16)
        res = jnp.dot(lhs, w, preferred_element_type=_F32)
        p_front[k] = res
        return res[rows - SUBLANES:rows]

    def conv_slab(j, start):
        n = rows // CONV_SLABS
        r0 = j * n
        s = r0 // seglen
        h, bg, cg, za = (p_back[k, r0:r0 + n, :] for k in range(4))
        xc = cg * _after(h, start)
        ext = jnp.concatenate([hist_buf[s], xc], axis=0)
        xm1 = pltpu.roll(ext, 1, axis=0)[SUBLANES:]
        xm2 = pltpu.roll(ext, 2, axis=0)[SUBLANES:]
        hist_buf[s] = xc[n - SUBLANES:n]
        cw = cw_ref[...]
        yc = cw[0:1] * xm2 + cw[1:2] * xm1 + cw[2:3] * xc
        ya = (_rms(bg * yc, gcv_ref[...]) * jax.nn.silu(za)).astype(_BF16)
        ycat_buf[r0:r0 + n, 0:E_CONV] = ya
        return ya

    def sgu_mix():
        v = p_back[5]
        mu = jnp.mean(v, axis=-1, keepdims=True)
        vc = v - mu
        var = jnp.mean(vc * vc, axis=-1, keepdims=True)
        vn = vc * lax.rsqrt(var + EPS) * lng_ref[...] + lnb_ref[...]
        if vn_ref is not None:
            vn_ref[...] = vn
        vnb = vn.astype(_BF16)
        for g in range(G_SGU):
            wg = wsg_buf[g, 0:ch, 0:ch]
            rhs = jnp.concatenate(
                [vnb[c * ch:(c + 1) * ch, g * SGU_HEAD:(g + 1) * SGU_HEAD]
                 for c in range(nch)], axis=1)
            m = jnp.dot(wg, rhs, preferred_element_type=_F32)
            for c in range(nch):
                mixed_buf[c * ch:(c + 1) * ch, g * SGU_HEAD:(g + 1) * SGU_HEAD] = (
                    m[:, c * SGU_HEAD:(c + 1) * SGU_HEAD])

    def gate_slab(j, start):
        n = rows // GATE_SLABS
        r0 = j * n
        u, zb = p_back[4, r0:r0 + n, :], p_back[6, r0:r0 + n, :]
        u = _after(u, start)
        bias = jnp.concatenate([sgb_ref[0:ch, :]] * (n // ch), axis=0)
        mixed = mixed_buf[r0:r0 + n, :] + bias
        yb = (_rms(u * mixed, gsg_ref[...]) * jax.nn.silu(zb)).astype(_BF16)
        ycat_buf[r0:r0 + n, E_CONV:E_CONV + E_SGU] = yb
        return yb

    carried = hist_buf[:, hist:SUBLANES, :]
    hist_buf[:, hist:SUBLANES, :] = jnp.where(seq_start, prev_ref[...], carried)

    tails = [project(0)]
    sgu_mix()
    tails.append(project(1))
    for j in range(CONV_SLABS):
        tails.append(project(2 + j, done=conv_slab(j, tails[j])))
    cs_ref[...] = hist_buf[:, hist:SUBLANES, :]

    def scale_next(start):
        scale = _row_scale(_after(xnext_ref[...], start))
        scale_buf[...] = scale
        return scale

    tails.append(project(6, done=[gate_slab(0, tails[4]), scale_next(tails[4])]))
    gate_slab(1, tails[5])

    out = xb_ref[...] + jnp.dot(ycat_buf[...], pltpu.bitcast(wout_ref[...], _BF16),
                               preferred_element_type=_F32)
    if final:
        out = _rms(out, fing_ref[...])
    y_ref[...] = out


def _layer_kernel(xf_ref, xnext_ref, xb_ref, prev_ref, ng_ref, win_ref, cw_ref, lng_ref, lnb_ref,
                  sgw_ref, sgb_ref, gcv_ref, gsg_ref, wout_ref, fing_ref,
                  y_ref, cs_ref, *rest, tiles_per_seq, emit_vn, **static):
    if emit_vn:
        vn_ref, *scratch = rest
    else:
        vn_ref, scratch = None, rest
    hist_buf, mixed_buf, ycat_buf, wsg_buf, scale_buf, p_even, p_odd = scratch
    i = pl.program_id(0)
    seq_start = jnp.logical_or(i == 0, (i - 1) % tiles_per_seq == 0)

    @pl.when(i == 0)
    def _():
        p_odd[...] = jnp.zeros(p_odd.shape, _F32)
        hist_buf[...] = jnp.zeros(hist_buf.shape, _F32)
        scale_buf[...] = _row_scale(xf_ref[...])
        blk_t = lax.broadcasted_iota(jnp.int32, (SGU_CHUNK, SGU_CHUNK), 0) // CHUNK
        blk_s = lax.broadcasted_iota(jnp.int32, (SGU_CHUNK, SGU_CHUNK), 1) // CHUNK
        mask = (blk_t >= blk_s).astype(_F32)
        for g in range(G_SGU):
            wsg_buf[g] = (sgw_ref[g] * mask).astype(_BF16)

    step = functools.partial(
        _layer_step, xf_ref, xnext_ref, xb_ref, prev_ref, ng_ref, win_ref, cw_ref, lng_ref,
        lnb_ref, sgw_ref, sgb_ref, gcv_ref, gsg_ref, wout_ref, fing_ref, y_ref, cs_ref, vn_ref,
        hist_buf, mixed_buf, ycat_buf, wsg_buf, scale_buf, seq_start=seq_start, **static)

    @pl.when(i % 2 == 0)
    def _():
        step(p_front=p_even, p_back=p_odd)

    @pl.when(i % 2 == 1)
    def _():
        step(p_front=p_odd, p_back=p_even)


def _resident(shape):
    return pl.BlockSpec(shape, lambda i: (0,) * len(shape), pipeline_mode=pl.Buffered(1))


def _resident_layer(shape, l):
    return pl.BlockSpec((None, *shape), lambda i: (l,) + (0,) * len(shape),
                        pipeline_mode=pl.Buffered(1))


def _pack_kernel(w_ref, o_ref):
    o_ref[...] = pltpu.bitcast(w_ref[...].astype(_BF16), jnp.uint32)


def _pack_rows(w):
    depth, k, n = w.shape
    bk = k
    while bk * n * w.dtype.itemsize > PACK_BLOCK_BYTES and bk % (4 * BF16_ROWS) == 0:
        bk //= 2
    return pl.pallas_call(
        _pack_kernel,
        grid=(depth, k // bk),
        in_specs=[pl.BlockSpec((None, bk, n), lambda l, j: (l, j, 0))],
        out_specs=pl.BlockSpec((None, bk // 2, n), lambda l, j: (l, j, 0)),
        out_shape=jax.ShapeDtypeStruct((depth, k // 2, n), jnp.uint32),
        name="pack_rows",
    )(w)


def _layer(x, prev, lw, *, layer, nseg, seglen, ch, tiles_per_seq, final, emit_vn, name):
    n_rows = x.shape[0]
    rows = nseg * seglen
    n_tiles = n_rows // rows
    assert n_tiles * rows == n_rows and (nseg == 1 or tiles_per_seq == 1)
    assert prev.shape[0] * tiles_per_seq == n_tiles
    conv_slab, gate_slab = rows // CONV_SLABS, rows // GATE_SLABS
    assert seglen % conv_slab == 0 and gate_slab % ch == 0 and rows % ch == 0
    mixed_tile = lambda i: jnp.maximum(i - 1, 0)
    front_spec = pl.BlockSpec((rows, D_MODEL), lambda i: (jnp.minimum(i, n_tiles - 1), 0))
    next_spec = pl.BlockSpec((rows, D_MODEL), lambda i: (jnp.minimum(i + 1, n_tiles - 1), 0))
    back_spec = pl.BlockSpec((rows, D_MODEL), lambda i: (mixed_tile(i), 0))
    state_spec = pl.BlockSpec((None, nseg, CONV_W - 1, E_CONV),
                              lambda i: (mixed_tile(i) // tiles_per_seq, 0, 0, 0))
    out_shape = [jax.ShapeDtypeStruct(x.shape, _F32),
                 jax.ShapeDtypeStruct(prev.shape, _F32)]
    out_specs = [back_spec, state_spec]
    if emit_vn:
        out_shape.append(jax.ShapeDtypeStruct((n_rows, E_SGU), _F32))
        out_specs.append(back_spec)
    kern = functools.partial(_layer_kernel, nseg=nseg, seglen=seglen, ch=ch,
                             tiles_per_seq=tiles_per_seq, final=final, emit_vn=emit_vn)
    proj_buf = pltpu.VMEM((N_PROJ, rows, E_CONV), _F32)
    return pl.pallas_call(
        kern,
        grid=(n_tiles + 1,),
        in_specs=[
            front_spec, next_spec, back_spec, state_spec,
            _resident((1, D_MODEL)),
            _resident_layer((D_MODEL // 2, N_PROJ * E_CONV), layer),
            _resident((CONV_W, E_CONV)),
            _resident((1, E_SGU)), _resident((1, E_SGU)),
            _resident((G_SGU, SGU_CHUNK, SGU_CHUNK)),
            _resident((SGU_CHUNK, E_SGU)),
            _resident((1, E_CONV)), _resident((1, E_SGU)),
            _resident_layer(((E_CONV + E_SGU) // 2, D_MODEL), layer),
            _resident((1, D_MODEL)),
        ],
        out_specs=out_specs,
        out_shape=out_shape,
        scratch_shapes=[pltpu.VMEM((nseg, SUBLANES, E_CONV), _F32),
                        pltpu.VMEM((rows, E_SGU), _F32),
                        pltpu.VMEM((rows, E_CONV + E_SGU), _BF16),
                        pltpu.VMEM((G_SGU, SGU_CHUNK, SGU_CHUNK), _BF16),
                        pltpu.VMEM((rows, 1), _F32),
                        proj_buf, proj_buf],
        compiler_params=pltpu.CompilerParams(
            dimension_semantics=("arbitrary",),
            vmem_limit_bytes=VMEM_LIMIT_BYTES),
        name=name,
    )(x, x, x, prev, *lw)


def kernel(x_prompt, x_sample, state_conv, norm_g, w_in, conv_w, sg_ln_g, sg_ln_b, sg_w, sg_b,
           out_g_conv, out_g_sgu, w_out, final_g):
    depth = w_in.shape[0]
    batch, seq, _ = x_prompt.shape
    dec_batch, dec_seq, _ = x_sample.shape
    assert dec_seq <= CHUNK and TILE_ROWS % dec_seq == 0 and seq % TILE_ROWS == 0
    seqs_per_tile = TILE_ROWS // dec_seq
    dec_groups = dec_batch // seqs_per_tile

    w_in_p, w_out_p = _pack_rows(w_in), _pack_rows(w_out)

    def layer_weights(l):
        row = lambda a: a[l].reshape(1, -1)
        bias = jnp.repeat(sg_b[l].T, SGU_HEAD, axis=1)
        return (row(norm_g), w_in_p, conv_w[l], row(sg_ln_g), row(sg_ln_b),
                sg_w[l], bias, row(out_g_conv), row(out_g_sgu), w_out_p,
                final_g.reshape(1, -1))

    xp = x_prompt.reshape(batch * seq, D_MODEL)
    xs = x_sample.reshape(dec_batch * dec_seq, D_MODEL)
    zero_prev = jnp.zeros((batch, 1, CONV_W - 1, E_CONV), _F32)
    conv_p, conv_s, v_s = [], [], []
    for l in range(depth):
        lw = layer_weights(l)
        final = l == depth - 1
        xp, cs = _layer(xp, zero_prev, lw, layer=l, nseg=1, seglen=TILE_ROWS, ch=SGU_CHUNK,
                        tiles_per_seq=seq // TILE_ROWS, final=final, emit_vn=False,
                        name=f"prompt_layer{l}")
        conv_p.append(cs.reshape(batch, CONV_W - 1, E_CONV))
        prev = state_conv[l].reshape(dec_groups, seqs_per_tile, CONV_W - 1, E_CONV)
        xs, cs, vn = _layer(xs, prev, lw, layer=l, nseg=seqs_per_tile, seglen=dec_seq,
                            ch=dec_seq, tiles_per_seq=1, final=final, emit_vn=True,
                            name=f"sample_layer{l}")
        conv_s.append(cs.reshape(dec_batch, CONV_W - 1, E_CONV))
        v_s.append(vn.reshape(dec_batch, dec_seq, E_SGU))
    return (xp.reshape(x_prompt.shape), xs.reshape(x_sample.shape), jnp.stack(conv_p, axis=0),
            jnp.stack(conv_s, axis=0), jnp.stack(v_s, axis=0))
```

```python
import functools

import jax
import jax.numpy as jnp
from jax import lax
from jax.experimental import pallas as pl
from jax.experimental.pallas import tpu as pltpu

D_MODEL = 1024
E_CONV = 1024
E_SGU = 1024
G_SGU = 8
SGU_HEAD = E_SGU // G_SGU
SGU_CHUNK = 128
CHUNK = 64
CONV_W = 3
N_PROJ = 7
EPS = 1e-6

SUBLANES = 8
BF16_ROWS = 16
TILE_ROWS = 256
CONV_SLABS = 4
GATE_SLABS = 2
VMEM_LIMIT_BYTES = 56 * 1024 * 1024
PACK_BLOCK_BYTES = 4 * 1024 * 1024

_F32 = jnp.float32
_BF16 = jnp.bfloat16


def _rms(v, g):
    return v * lax.rsqrt(jnp.mean(v * v, axis=-1, keepdims=True) + EPS) * g


def _after(v, done):
    if done is None:
        return v
    bits = pltpu.bitcast(done, jnp.int32)
    parts = [bits[j:j + SUBLANES] for j in range(0, bits.shape[0], SUBLANES)]
    while len(parts) > 1:
        parts = [a | b for a, b in zip(parts[0::2], parts[1::2])] + parts[len(parts) & ~1:]
    zero = lax.shift_right_logical(lax.shift_right_logical(parts[0], 16), 16).astype(_F32)
    return v + jnp.concatenate([zero] * (v.shape[0] // SUBLANES), axis=0)


def _layer_step(xf_ref, xb_ref, prev_ref, ng_ref, win_ref, cw_ref, lng_ref, lnb_ref,
                sgw_ref, sgb_ref, gcv_ref, gsg_ref, wout_ref, fing_ref,
                y_ref, cs_ref, vn_ref, hist_buf, mixed_buf, ycat_buf, p_front, p_back,
                seq_start, *, nseg, seglen, ch, final):
    rows = nseg * seglen
    hist = SUBLANES - (CONV_W - 1)
    nch = rows // ch

    if p_front is not None:
        xn = _rms(xf_ref[...], ng_ref[...])
        xn_head = xn[0:BF16_ROWS]
        xn_tail = xn[BF16_ROWS:].astype(_BF16)

    def project(k, done=None):
        if p_front is None:
            return None
        head = xn_head if done is None else _after(xn_head, done)
        lhs = jnp.concatenate([head.astype(_BF16), xn_tail], axis=0)
        w = pltpu.bitcast(win_ref[:, k * E_CONV:(k + 1) * E_CONV], _BF16)
        res = jnp.dot(lhs, w, preferred_element_type=_F32)
        p_front[k] = res
        return res[rows - SUBLANES:rows]

    def conv_slab(j, start):
        n = rows // CONV_SLABS
        r0 = j * n
        s = r0 // seglen
        h, bg, cg, za = (p_back[k, r0:r0 + n, :] for k in range(4))
        xc = cg * _after(h, start)
        ext = jnp.concatenate([hist_buf[s], xc], axis=0)
        xm1 = pltpu.roll(ext, 1, axis=0)[SUBLANES:]
        xm2 = pltpu.roll(ext, 2, axis=0)[SUBLANES:]
        hist_buf[s] = xc[n - SUBLANES:n]
        cw = cw_ref[...]
        yc = cw[0:1] * xm2 + cw[1:2] * xm1 + cw[2:3] * xc
        ya = (_rms(bg * yc, gcv_ref[...]) * jax.nn.silu(za)).astype(_BF16)
        ycat_buf[r0:r0 + n, 0:E_CONV] = ya
        return ya

    def sgu_mix():
        v = p_back[5]
        mu = jnp.mean(v, axis=-1, keepdims=True)
        vc = v - mu
        var = jnp.mean(vc * vc, axis=-1, keepdims=True)
        vn = vc * lax.rsqrt(var + EPS) * lng_ref[...] + lnb_ref[...]
        if vn_ref is not None:
            vn_ref[...] = vn
        vnb = vn.astype(_BF16)
        blk_t = lax.broadcasted_iota(jnp.int32, (ch, ch), 0) // CHUNK
        blk_s = lax.broadcasted_iota(jnp.int32, (ch, ch), 1) // CHUNK
        mask = (blk_t >= blk_s).astype(_F32)
        for g in range(G_SGU):
            wg = (sgw_ref[g, 0:ch, 0:ch] * mask).astype(_BF16)
            rhs = jnp.concatenate(
                [vnb[c * ch:(c + 1) * ch, g * SGU_HEAD:(g + 1) * SGU_HEAD]
                 for c in range(nch)], axis=1)
            m = jnp.dot(wg, rhs, preferred_element_type=_F32)
            for c in range(nch):
                mixed_buf[c * ch:(c + 1) * ch, g * SGU_HEAD:(g + 1) * SGU_HEAD] = (
                    m[:, c * SGU_HEAD:(c + 1) * SGU_HEAD])

    def gate_slab(j, start):
        n = rows // GATE_SLABS
        r0 = j * n
        u, zb = p_back[4, r0:r0 + n, :], p_back[6, r0:r0 + n, :]
        u = _after(u, start)
        bias = jnp.concatenate([sgb_ref[0:ch, :]] * (n // ch), axis=0)
        mixed = mixed_buf[r0:r0 + n, :] + bias
        yb = (_rms(u * mixed, gsg_ref[...]) * jax.nn.silu(zb)).astype(_BF16)
        ycat_buf[r0:r0 + n, E_CONV:E_CONV + E_SGU] = yb
        return yb

    carried = hist_buf[:, hist:SUBLANES, :]
    hist_buf[:, hist:SUBLANES, :] = jnp.where(seq_start, prev_ref[...], carried)

    tails = [project(0)]
    sgu_mix()
    tails.append(project(1))
    for j in range(CONV_SLABS):
        tails.append(project(2 + j, done=conv_slab(j, tails[j])))
    cs_ref[...] = hist_buf[:, hist:SUBLANES, :]
    tails.append(project(6, done=gate_slab(0, tails[4])))
    gate_slab(1, tails[5])

    out = xb_ref[...] + jnp.dot(ycat_buf[...], pltpu.bitcast(wout_ref[...], _BF16),
                               preferred_element_type=_F32)
    if final:
        out = _rms(out, fing_ref[...])
    y_ref[...] = out


def _layer_kernel(xf_ref, xb_ref, prev_ref, ng_ref, win_ref, cw_ref, lng_ref, lnb_ref,
                  sgw_ref, sgb_ref, gcv_ref, gsg_ref, wout_ref, fing_ref,
                  y_ref, cs_ref, *rest, n_tiles, tiles_per_seq, emit_vn, **static):
    if emit_vn:
        vn_ref, *scratch = rest
    else:
        vn_ref, scratch = None, rest
    hist_buf, mixed_buf, ycat_buf, p_even, p_odd = scratch
    i = pl.program_id(0)
    seq_start = jnp.logical_or(i == 0, (i - 1) % tiles_per_seq == 0)

    @pl.when(i == 0)
    def _():
        p_odd[...] = jnp.zeros(p_odd.shape, _F32)
        hist_buf[...] = jnp.zeros(hist_buf.shape, _F32)

    step = functools.partial(
        _layer_step, xf_ref, xb_ref, prev_ref, ng_ref, win_ref, cw_ref, lng_ref, lnb_ref,
        sgw_ref, sgb_ref, gcv_ref, gsg_ref, wout_ref, fing_ref, y_ref, cs_ref, vn_ref,
        hist_buf, mixed_buf, ycat_buf, seq_start=seq_start, **static)

    @pl.when(jnp.logical_and(i % 2 == 0, i < n_tiles))
    def _():
        step(p_front=p_even, p_back=p_odd)

    @pl.when(jnp.logical_and(i % 2 == 1, i < n_tiles))
    def _():
        step(p_front=p_odd, p_back=p_even)

    @pl.when(i == n_tiles)
    def _():
        step(p_front=None, p_back=p_odd if n_tiles % 2 == 0 else p_even)


def _resident(shape):
    return pl.BlockSpec(shape, lambda i: (0,) * len(shape), pipeline_mode=pl.Buffered(1))


def _resident_layer(shape, l):
    return pl.BlockSpec((None, *shape), lambda i: (l,) + (0,) * len(shape),
                        pipeline_mode=pl.Buffered(1))


def _pack_kernel(w_ref, o_ref):
    o_ref[...] = pltpu.bitcast(w_ref[...].astype(_BF16), jnp.uint32)


def _pack_rows(w):
    depth, k, n = w.shape
    bk = k
    while bk * n * w.dtype.itemsize > PACK_BLOCK_BYTES and bk % (4 * BF16_ROWS) == 0:
        bk //= 2
    return pl.pallas_call(
        _pack_kernel,
        grid=(depth, k // bk),
        in_specs=[pl.BlockSpec((None, bk, n), lambda l, j: (l, j, 0))],
        out_specs=pl.BlockSpec((None, bk // 2, n), lambda l, j: (l, j, 0)),
        out_shape=jax.ShapeDtypeStruct((depth, k // 2, n), jnp.uint32),
        name="pack_rows",
    )(w)


def _layer(x, prev, lw, *, layer, nseg, seglen, ch, tiles_per_seq, final, emit_vn, name):
    n_rows = x.shape[0]
    rows = nseg * seglen
    n_tiles = n_rows // rows
    assert n_tiles * rows == n_rows and (nseg == 1 or tiles_per_seq == 1)
    assert prev.shape[0] * tiles_per_seq == n_tiles
    conv_slab, gate_slab = rows // CONV_SLABS, rows // GATE_SLABS
    assert seglen % conv_slab == 0 and gate_slab % ch == 0 and rows % ch == 0
    mixed_tile = lambda i: jnp.maximum(i - 1, 0)
    front_spec = pl.BlockSpec((rows, D_MODEL), lambda i: (jnp.minimum(i, n_tiles - 1), 0))
    back_spec = pl.BlockSpec((rows, D_MODEL), lambda i: (mixed_tile(i), 0))
    state_spec = pl.BlockSpec((None, nseg, CONV_W - 1, E_CONV),
                              lambda i: (mixed_tile(i) // tiles_per_seq, 0, 0, 0))
    out_shape = [jax.ShapeDtypeStruct(x.shape, _F32),
                 jax.ShapeDtypeStruct(prev.shape, _F32)]
    out_specs = [back_spec, state_spec]
    if emit_vn:
        out_shape.append(jax.ShapeDtypeStruct((n_rows, E_SGU), _F32))
        out_specs.append(back_spec)
    kern = functools.partial(_layer_kernel, nseg=nseg, seglen=seglen, ch=ch, n_tiles=n_tiles,
                             tiles_per_seq=tiles_per_seq, final=final, emit_vn=emit_vn)
    proj_buf = pltpu.VMEM((N_PROJ, rows, E_CONV), _F32)
    return pl.pallas_call(
        kern,
        grid=(n_tiles + 1,),
        in_specs=[
            front_spec, back_spec, state_spec,
            _resident((1, D_MODEL)),
            _resident_layer((D_MODEL // 2, N_PROJ * E_CONV), layer),
            _resident((CONV_W, E_CONV)),
            _resident((1, E_SGU)), _resident((1, E_SGU)),
            _resident((G_SGU, SGU_CHUNK, SGU_CHUNK)),
            _resident((SGU_CHUNK, E_SGU)),
            _resident((1, E_CONV)), _resident((1, E_SGU)),
            _resident_layer(((E_CONV + E_SGU) // 2, D_MODEL), layer),
            _resident((1, D_MODEL)),
        ],
        out_specs=out_specs,
        out_shape=out_shape,
        scratch_shapes=[pltpu.VMEM((nseg, SUBLANES, E_CONV), _F32),
                        pltpu.VMEM((rows, E_SGU), _F32),
                        pltpu.VMEM((rows, E_CONV + E_SGU), _BF16),
                        proj_buf, proj_buf],
        compiler_params=pltpu.CompilerParams(
            dimension_semantics=("arbitrary",),
            vmem_limit_bytes=VMEM_LIMIT_BYTES),
        name=name,
    )(x, x, prev, *lw)


def kernel(x_prompt, x_sample, state_conv, norm_g, w_in, conv_w, sg_ln_g, sg_ln_b, sg_w, sg_b,
           out_g_conv, out_g_sgu, w_out, final_g):
    depth = w_in.shape[0]
    batch, seq, _ = x_prompt.shape
    dec_batch, dec_seq, _ = x_sample.shape
    assert dec_seq <= CHUNK and TILE_ROWS % dec_seq == 0 and seq % TILE_ROWS == 0
    seqs_per_tile = TILE_ROWS // dec_seq
    dec_groups = dec_batch // seqs_per_tile

    w_in_p, w_out_p = _pack_rows(w_in), _pack_rows(w_out)

    def layer_weights(l):
        row = lambda a: a[l].reshape(1, -1)
        bias = jnp.repeat(sg_b[l].T, SGU_HEAD, axis=1)
        return (row(norm_g), w_in_p, conv_w[l], row(sg_ln_g), row(sg_ln_b),
                sg_w[l], bias, row(out_g_conv), row(out_g_sgu), w_out_p,
                final_g.reshape(1, -1))

    xp = x_prompt.reshape(batch * seq, D_MODEL)
    xs = x_sample.reshape(dec_batch * dec_seq, D_MODEL)
    zero_prev = jnp.zeros((batch, 1, CONV_W - 1, E_CONV), _F32)
    conv_p, conv_s, v_s = [], [], []
    for l in range(depth):
        lw = layer_weights(l)
        final = l == depth - 1
        xp, cs = _layer(xp, zero_prev, lw, layer=l, nseg=1, seglen=TILE_ROWS, ch=SGU_CHUNK,
                        tiles_per_seq=seq // TILE_ROWS, final=final, emit_vn=False,
                        name=f"prompt_layer{l}")
        conv_p.append(cs.reshape(batch, CONV_W - 1, E_CONV))
        prev = state_conv[l].reshape(dec_groups, seqs_per_tile, CONV_W - 1, E_CONV)
        xs, cs, vn = _layer(xs, prev, lw, layer=l, nseg=seqs_per_tile, seglen=dec_seq,
                            ch=dec_seq, tiles_per_seq=1, final=final, emit_vn=True,
                            name=f"sample_layer{l}")
        conv_s.append(cs.reshape(dec_batch, CONV_W - 1, E_CONV))
        v_s.append(vn.reshape(dec_batch, dec_seq, E_SGU))
    return (xp.reshape(x_prompt.shape), xs.reshape(x_sample.shape), jnp.stack(conv_p, axis=0),
            jnp.stack(conv_s, axis=0), jnp.stack(v_s, axis=0))
```
